```python
import jax, jax.numpy as jnp
from jax import lax
import numpy as np

D_MODEL = 1024
BATCH = 2
SEQ = 8192
DEPTH = 1

PLE_DIM = 256
D_FF = 2816
HG_HEADS = 8
HG_DK = 128
HG_DV = 128
HG_WIDTH = HG_HEADS * HG_DK
HG_VWIDTH = HG_HEADS * HG_DV
CHUNK = 64
POOL_WINDOWS = (2, 4, 8, 16)
POOL_GROUPS = 4
POOL_CH = 128
POOL_WIDTH = POOL_GROUPS * POOL_CH
IN_SIZES = (HG_WIDTH, HG_WIDTH, HG_VWIDTH, HG_VWIDTH, POOL_WIDTH, D_MODEL, D_MODEL)
IN_COLS = HG_WIDTH * 2 + HG_VWIDTH * 2 + POOL_WIDTH + 2 * D_MODEL
EPS = 1e-6

kernel_name = "hybrid_hgrn2_pool_macaron_block"


def _rmsnorm(x, g):
    xf = x.astype(jnp.float32)
    y = xf * lax.rsqrt(jnp.mean(xf * xf, axis=-1, keepdims=True) + EPS)
    return (y * g.astype(jnp.float32)).astype(x.dtype)


def _swiglu(h, w1, w3, w2):
    return (jax.nn.silu(h @ w1) * (h @ w3)) @ w2


def _hgrn2_chunked(q, k, v, log_f):
    B, S, H, DK = q.shape
    DV = v.shape[-1]
    n_chunks = S // CHUNK

    def to_chunks(t):
        return t.reshape(B, n_chunks, CHUNK, H, t.shape[-1]).transpose(1, 0, 3, 2, 4)

    qc, kc, vc, gc = to_chunks(q), to_chunks(k), to_chunks(v), to_chunks(log_f)
    causal = jnp.tril(jnp.ones((CHUNK, CHUNK), dtype=bool))[:, :, None]

    def step(state, inp):
        qb, kb, vb, gb = inp
        G = jnp.cumsum(gb, axis=2)
        diff = G[:, :, :, None, :] - G[:, :, None, :, :]
        decay = jnp.exp(jnp.where(causal, diff, -jnp.inf))
        scores = jnp.einsum('bhtk,bhsk,bhtsk->bhts', qb, kb, decay)
        o_intra = jnp.einsum('bhts,bhsv->bhtv', scores, vb)
        o_inter = jnp.einsum('bhtk,bhkv->bhtv', qb * jnp.exp(G), state)
        G_last = G[:, :, -1:, :]
        k_dec = kb * jnp.exp(G_last - G)
        new_state = (jnp.exp(G_last[:, :, 0, :])[..., None] * state
                     + jnp.einsum('bhsk,bhsv->bhkv', k_dec, vb))
        return new_state, o_intra + o_inter

    state0 = jnp.zeros((B, H, DK, DV), jnp.float32)
    _, oc = lax.scan(step, state0, (qc, kc, vc, gc))
    return oc.transpose(1, 0, 3, 2, 4).reshape(B, S, H, DV)


def _causal_multiscale_pool(u):
    B, S, G, C = u.shape
    uf = u.astype(jnp.float32)
    cs = jnp.concatenate([jnp.zeros((B, 1, G, C), jnp.float32), jnp.cumsum(uf, axis=1)], axis=1)
    pos = jnp.arange(1, S + 1, dtype=jnp.float32)
    outs = []
    for g, w in enumerate(POOL_WINDOWS):
        csg = cs[:, :, g]
        upper = csg[:, 1:]
        lower = jnp.concatenate([jnp.zeros((B, w - 1, C), jnp.float32), csg[:, :S - w + 1]], axis=1)
        count = jnp.minimum(pos, float(w))[None, :, None]
        outs.append((upper - lower) / count - uf[:, :, g])
    return jnp.stack(outs, axis=2).astype(u.dtype)


def _normal(key, shape, fan_in):
    return jax.random.normal(key, shape, jnp.float32) * (fan_in ** -0.5)


def _gain(key, shape):
    return 1.0 + 0.02 * jax.random.normal(key, shape, jnp.float32)


def setup_inputs(seed: int = 0) -> dict:
    key = jax.random.key(seed)
    ks = jax.random.split(key, 26)
    L = DEPTH
    return {
        "x": jax.random.normal(ks[0], (BATCH, SEQ, D_MODEL), jnp.float32),
        "p": jax.random.normal(ks[1], (DEPTH, BATCH, SEQ, PLE_DIM), jnp.float32),
        "ffn1_norm": _gain(ks[2], (L, D_MODEL)),
        "ffn1_w1": _normal(ks[3], (L, D_MODEL, D_FF), D_MODEL),
        "ffn1_w3": _normal(ks[4], (L, D_MODEL, D_FF), D_MODEL),
        "ffn1_w2": _normal(ks[5], (L, D_FF, D_MODEL), D_FF),
        "mix_norm": _gain(ks[6], (L, D_MODEL)),
        "w_in": _normal(ks[7], (L, D_MODEL, IN_COLS), D_MODEL),
        "hgrn_lb": 0.1 * jax.random.normal(ks[8], (L + 1, HG_WIDTH), jnp.float32),
        "hgrn_onorm": _gain(ks[9], (L, HG_VWIDTH)),
        "w_branch_a": _normal(ks[10], (L, HG_VWIDTH, D_MODEL), HG_VWIDTH),
        "pool_w": _normal(ks[11], (L, POOL_GROUPS, POOL_CH, POOL_CH), POOL_CH),
        "pool_scale": _gain(ks[12], (L, POOL_WIDTH)),
        "w_branch_b": _normal(ks[13], (L, POOL_WIDTH, D_MODEL), POOL_WIDTH),
        "w_out": _normal(ks[14], (L, D_MODEL, D_MODEL), D_MODEL),
        "ffn2_norm": _gain(ks[15], (L, D_MODEL)),
        "ffn2_w1": _normal(ks[16], (L, D_MODEL, D_FF), D_MODEL),
        "ffn2_w3": _normal(ks[17], (L, D_MODEL, D_FF), D_MODEL),
        "ffn2_w2": _normal(ks[18], (L, D_FF, D_MODEL), D_FF),
        "ple_norm": _gain(ks[19], (L, D_MODEL)),
        "ple_w_gate": _normal(ks[20], (L, D_MODEL, D_MODEL), D_MODEL),
        "ple_w_proj": _normal(ks[21], (L, PLE_DIM, D_MODEL), PLE_DIM),
        "ple_post_norm": _gain(ks[22], (L, D_MODEL)),
        "final_norm": _gain(ks[23], (D_MODEL,)),
    }


def reference(x, p, ffn1_norm, ffn1_w1, ffn1_w3, ffn1_w2, mix_norm, w_in, hgrn_lb, hgrn_onorm,
              w_branch_a, pool_w, pool_scale, w_branch_b, w_out, ffn2_norm, ffn2_w1, ffn2_w3,
              ffn2_w2, ple_norm, ple_w_gate, ple_w_proj, ple_post_norm, final_norm):
    B, S, _ = x.shape
    split_points = [int(v) for v in np.cumsum(IN_SIZES)[:-1]]
    lb_all = jnp.cumsum(jax.nn.softmax(hgrn_lb.astype(jnp.float32), axis=0), axis=0)

    for i in range(DEPTH):
        h = _rmsnorm(x, ffn1_norm[i])
        x = x + 0.5 * _swiglu(h, ffn1_w1[i], ffn1_w3[i], ffn1_w2[i])

        h = _rmsnorm(x, mix_norm[i])
        proj = h @ w_in[i]
        q_r, f_r, i_r, og_r, pool_r, ga_r, gb_r = jnp.split(proj, split_points, axis=-1)

        lb = lb_all[i]
        f = lb + (1.0 - lb) * jax.nn.sigmoid(f_r.astype(jnp.float32))
        log_f = jnp.log(f).reshape(B, S, HG_HEADS, HG_DK)
        k = (1.0 - f).reshape(B, S, HG_HEADS, HG_DK)
        q = jax.nn.silu(q_r.astype(jnp.float32)).reshape(B, S, HG_HEADS, HG_DK)
        v = i_r.astype(jnp.float32).reshape(B, S, HG_HEADS, HG_DV)
        o = _hgrn2_chunked(q, k, v, log_f).astype(x.dtype)
        o = _rmsnorm(o, hgrn_onorm[i].reshape(HG_HEADS, HG_DV)) * jax.nn.silu(og_r.reshape(B, S, HG_HEADS, HG_DV))
        y_a = o.reshape(B, S, HG_VWIDTH) @ w_branch_a[i]

        u = pool_r.reshape(B, S, POOL_GROUPS, POOL_CH)
        pooled = _causal_multiscale_pool(u)
        mixed = jnp.einsum('bsgc,gcd->bsgd', pooled, pool_w[i]).reshape(B, S, POOL_WIDTH) * pool_scale[i]
        y_b = mixed @ w_branch_b[i]

        y = jax.nn.sigmoid(ga_r) * y_a + jax.nn.sigmoid(gb_r) * y_b
        x = x + y @ w_out[i]

        h = _rmsnorm(x, ffn2_norm[i])
        x = x + 0.5 * _swiglu(h, ffn2_w1[i], ffn2_w3[i], ffn2_w2[i])

        gate = jax.nn.sigmoid(_rmsnorm(x, ple_norm[i]) @ ple_w_gate[i])
        e = _rmsnorm(p[i] @ ple_w_proj[i], ple_post_norm[i])
        x = x + gate * e

    return _rmsnorm(x, final_norm)
```

```python
import functools

import numpy as np
import jax
import jax.numpy as jnp
from jax import lax
from jax.experimental import pallas as pl
from jax.experimental.pallas import tpu as pltpu

D_MODEL = 1024
D_FF = 2816
PLE_DIM = 256
HEADS = 8
HEAD_DIM = 128
HG_WIDTH = HEADS * HEAD_DIM
POOL_WINDOWS = (2, 4, 8, 16)
POOL_CH = 128
POOL_WIDTH = len(POOL_WINDOWS) * POOL_CH
POOL_HALO = 16
IN_SPLITS = (0, 1024, 2048, 3072, 4096, 4608, 5632, 6656)
EPS = 1e-6

CHUNK = 64
LEVEL_HALVES = (1, 2, 4, 8, 16, 32)
N_LEVELS = len(LEVEL_HALVES) + 2

V7X_VMEM_BYTES = 64 * 1024 * 1024
VMEM_LIMIT = V7X_VMEM_BYTES - 8 * 1024 * 1024

BF16 = jnp.bfloat16
F32 = jnp.float32


def _rms(x, g):
    return x * lax.rsqrt(jnp.mean(x * x, axis=-1, keepdims=True) + EPS) * g


def _silu(x):
    return x * jax.nn.sigmoid(x)


def _dot(a, b):
    return jnp.dot(a, b, preferred_element_type=F32)


def _dot_nt(a, b):
    return lax.dot_general(a, b, (((1,), (1,)), ((), ())), preferred_element_type=F32)


def _const_spec(shape):
    zeros = (0,) * len(shape)
    return pl.BlockSpec(shape, lambda *_: zeros, pipeline_mode=pl.Buffered(1))


def _params(n_grid_dims, sequential=False):
    sem = ("arbitrary" if sequential else "parallel",) * n_grid_dims
    return pltpu.CompilerParams(dimension_semantics=sem, vmem_limit_bytes=VMEM_LIMIT)


def _ffn_kernel(x_ref, g_ref, w1_ref, w3_ref, w2_ref, o_ref):
    x = x_ref[...]
    h = _rms(x, g_ref[...]).astype(BF16)
    a = _dot(h, w1_ref[...])
    b = _dot(h, w3_ref[...])
    act = (_silu(a) * b).astype(BF16)
    o_ref[...] = x + 0.5 * _dot(act, w2_ref[...])


def _ffn(x, g, w1, w3, w2, tm=512):
    t = x.shape[0]
    row = pl.BlockSpec((tm, D_MODEL), lambda i: (i, 0))
    return pl.pallas_call(
        _ffn_kernel,
        grid=(t // tm,),
        in_specs=[row, _const_spec((1, D_MODEL)), _const_spec((D_MODEL, D_FF)),
                  _const_spec((D_MODEL, D_FF)), _const_spec((D_FF, D_MODEL))],
        out_specs=row,
        out_shape=jax.ShapeDtypeStruct((t, D_MODEL), F32),
        compiler_params=_params(1),
        name="ffn",
    )(x, g, w1, w3, w2)


def _proj_kernel(x_ref, g_ref, w_ref, lb_ref, q_ref, k_ref, v_ref, lf_ref, og_ref, u_ref, ga_ref, gb_ref):
    h = _rms(x_ref[...], g_ref[...]).astype(BF16)

    def seg(i):
        return _dot(h, w_ref[:, IN_SPLITS[i]:IN_SPLITS[i + 1]])

    q_ref[...] = _silu(seg(0))
    lbp = lb_ref[...]
    e = jnp.exp(lbp - jnp.max(lbp, axis=0, keepdims=True))
    lb = e[0:1, :] / jnp.sum(e, axis=0, keepdims=True)
    f = lb + (1.0 - lb) * jax.nn.sigmoid(seg(1))
    lf_ref[...] = jnp.log(f)
    k_ref[...] = 1.0 - f
    v_ref[...] = seg(2)
    og_ref[...] = _silu(seg(3))
    u_ref[...] = seg(4)
    ga_ref[...] = jax.nn.sigmoid(seg(5))
    gb_ref[...] = jax.nn.sigmoid(seg(6))


def _proj(x, g, w_in, hgrn_lb, tm=256):
    t = x.shape[0]
    row = pl.BlockSpec((tm, D_MODEL), lambda i: (i, 0))
    row_u = pl.BlockSpec((tm, POOL_WIDTH), lambda i: (i, 0))
    wide = jax.ShapeDtypeStruct((t, D_MODEL), F32)
    return pl.pallas_call(
        _proj_kernel,
        grid=(t // tm,),
        in_specs=[row, _const_spec((1, D_MODEL)), _const_spec((D_MODEL, IN_SPLITS[-1])),
                  _const_spec(hgrn_lb.shape)],
        out_specs=[row, row, row, row, row, row_u, row, row],
        out_shape=[wide, wide, wide, wide, wide, jax.ShapeDtypeStruct((t, POOL_WIDTH), F32), wide, wide],
        compiler_params=_params(1),
        name="proj",
    )(x, g, w_in, hgrn_lb)


def _level_tables():
    c = CHUNK
    sums = np.zeros((N_LEVELS, c, c), np.float32)
    masks = np.zeros((len(LEVEL_HALVES) + 1, c, c), np.float32)
    for li, b in enumerate(LEVEL_HALVES):
        for t in range(c):
            base = (t // (2 * b)) * 2 * b
            r = base + b - 1
            if t > r:
                sums[li, t, r + 1:t + 1] = 1.0
                masks[li, t, base:r + 1] = 1.0
            else:
                sums[li, t, t + 1:r + 1] = 1.0
    for t in range(c):
        sums[N_LEVELS - 2, t, :t + 1] = 1.0
        sums[N_LEVELS - 1, t, t + 1:] = 1.0
    masks[len(LEVEL_HALVES)] = np.eye(c, dtype=np.float32)
    return sums.reshape(N_LEVELS * c, c), masks


def _hgrn_kernel(q_ref, k_ref, v_ref, lf_ref, sums_ref, masks_ref, o_ref, st_ref, *, n_chunks):
    @pl.when(pl.program_id(1) == 0)
    def _():
        st_ref[...] = jnp.zeros_like(st_ref)

    row = lax.broadcasted_iota(jnp.int32, (CHUNK, 1), 0)

    def chunk_body(c, carry):
        r0 = pl.multiple_of(c * CHUNK, CHUNK)
        rows = pl.ds(r0, CHUNK)
        q = q_ref[rows, :]
        k = k_ref[rows, :]
        v = v_ref[rows, :]
        lf = lf_ref[rows, :]
        l1 = lf.astype(BF16)
        rem = lf - l1.astype(F32)
        l2 = rem.astype(BF16)
        l3 = (rem - l2.astype(F32)).astype(BF16)
        expo = _dot(sums_ref[...], jnp.concatenate([l1, l2, l3], axis=0))
        decay = jnp.exp(expo)

        def level(i):
            return decay[i * CHUNK:(i + 1) * CHUNK, :]

        xs = []
        for li, b in enumerate(LEVEL_HALVES):
            upper = (row % (2 * b)) >= b
            xs.append((jnp.where(upper, q, k) * level(li)).astype(BF16))
        q_state = (q * level(N_LEVELS - 2)).astype(BF16)
        k_end = (k * level(N_LEVELS - 1)).astype(BF16)
        chunk_decay = decay[(N_LEVELS - 1) * CHUNK - 1:(N_LEVELS - 1) * CHUNK, :]
        qb = q.astype(BF16)
        kb = k.astype(BF16)

        for h in range(HEADS):
            cols = slice(h * HEAD_DIM, (h + 1) * HEAD_DIM)
            scores = masks_ref[len(LEVEL_HALVES)] * _dot_nt(qb[:, cols], kb[:, cols])
            for li in range(len(LEVEL_HALVES)):
                xh = xs[li][:, cols]
                scores = scores + masks_ref[li] * _dot_nt(xh, xh)
            vh = v[:, cols]
            state_t = st_ref[h]
            o = _dot(scores.astype(BF16), vh.astype(BF16)) + _dot_nt(q_state[:, cols], state_t.astype(BF16))
            o_ref[rows, cols] = o
            st_ref[h] = chunk_decay[:, cols] * state_t + _dot(vh.T.astype(BF16), k_end[:, cols])
        return carry

    lax.fori_loop(0, n_chunks, chunk_body, 0)


def _hgrn(q, k, v, lf, batch, tt=256):
    t = q.shape[0]
    steps = t // batch // tt
    sums, masks = _level_tables()
    sums3 = jnp.asarray(np.concatenate([sums, sums, sums], axis=1), BF16)
    row = pl.BlockSpec((tt, HG_WIDTH), lambda b, s: (b * steps + s, 0))
    return pl.pallas_call(
        functools.partial(_hgrn_kernel, n_chunks=tt // CHUNK),
        grid=(batch, steps),
        in_specs=[row, row, row, row, _const_spec(sums3.shape), _const_spec(masks.shape)],
        out_specs=row,
        out_shape=jax.ShapeDtypeStruct((t, HG_WIDTH), F32),
        scratch_shapes=[pltpu.VMEM((HEADS, HEAD_DIM, HEAD_DIM), F32)],
        compiler_params=_params(2, sequential=True),
        name="hgrn",
    )(q, k, v, lf, sums3, jnp.asarray(masks))


def _mix_kernel(x_ref, o_ref, og_ref, u_ref, uprev_ref, ga_ref, gb_ref, onorm_ref, wa_ref, pw_ref, ps_ref,
                wb_ref, wo_ref, out_ref, ext_ref, *, tm):
    s = pl.program_id(1)
    o = o_ref[...]
    heads = []
    for h in range(HEADS):
        oh = o[:, h * HEAD_DIM:(h + 1) * HEAD_DIM]
        heads.append(oh * lax.rsqrt(jnp.mean(oh * oh, axis=-1, keepdims=True) + EPS))
    on = jnp.concatenate(heads, axis=-1) * onorm_ref[...] * og_ref[...]
    ya = _dot(on.astype(BF16), wa_ref[...])

    ext_ref[0:POOL_HALO, :] = jnp.where(s == 0, 0.0, uprev_ref[...])
    ext_ref[POOL_HALO:POOL_HALO + tm, :] = u_ref[...]
    pos = (s * tm + 1 + lax.broadcasted_iota(jnp.int32, (tm, POOL_CH), 0)).astype(F32)
    groups = []
    for g, w in enumerate(POOL_WINDOWS):
        cols = slice(g * POOL_CH, (g + 1) * POOL_CH)
        tok = ext_ref[POOL_HALO:POOL_HALO + tm, cols]
        win = tok
        for j in range(1, w):
            win = win + ext_ref[POOL_HALO - j:POOL_HALO - j + tm, cols]
        pooled = win / jnp.minimum(pos, float(w)) - tok
        groups.append(_dot(pooled.astype(BF16), pw_ref[g]))
    mixed = jnp.concatenate(groups, axis=-1) * ps_ref[...]
    yb = _dot(mixed.astype(BF16), wb_ref[...])

    y = ga_ref[...] * ya + gb_ref[...] * yb
    out_ref[...] = x_ref[...] + _dot(y.astype(BF16), wo_ref[...])


def _mix(x, o, og, u, ga, gb, onorm, wa, pw, ps, wb, wo, batch, tm=256):
    t = x.shape[0]
    steps = t // batch // tm
    halo_blocks = tm // POOL_HALO
    row = pl.BlockSpec((tm, D_MODEL), lambda b, s: (b * steps + s, 0))
    row_u = pl.BlockSpec((tm, POOL_WIDTH), lambda b, s: (b * steps + s, 0))
    prev_u = pl.BlockSpec((POOL_HALO, POOL_WIDTH),
                          lambda b, s: (jnp.maximum((b * steps + s) * halo_blocks - 1, 0), 0))
    return pl.pallas_call(
        functools.partial(_mix_kernel, tm=tm),
        grid=(batch, steps),
        in_specs=[row, row, row, row_u, prev_u, row, row,
                  _const_spec((1, HG_WIDTH)), _const_spec((HG_WIDTH, D_MODEL)),
                  _const_spec((len(POOL_WINDOWS), POOL_CH, POOL_CH)), _const_spec((1, POOL_WIDTH)),
                  _const_spec((POOL_WIDTH, D_MODEL)), _const_spec((D_MODEL, D_MODEL))],
        out_specs=row,
        out_shape=jax.ShapeDtypeStruct((t, D_MODEL), F32),
        scratch_shapes=[pltpu.VMEM((POOL_HALO + tm, POOL_WIDTH), F32)],
        compiler_params=_params(2),
        name="mix",
    )(x, o, og, u, u, ga, gb, onorm, wa, pw, ps, wb, wo)


def _ple_kernel(x_ref, p_ref, gn_ref, wg_ref, wp_ref, pn_ref, fn_ref, o_ref):
    x = x_ref[...]
    gate = jax.nn.sigmoid(_dot(_rms(x, gn_ref[...]).astype(BF16), wg_ref[...]))
    e = _rms(_dot(p_ref[...].astype(BF16), wp_ref[...]), pn_ref[...])
    o_ref[...] = _rms(x + gate * e, fn_ref[...])


def _ple(x, p, gn, wg, wp, pn, fn, tm=512):
    t = x.shape[0]
    row = pl.BlockSpec((tm, D_MODEL), lambda i: (i, 0))
    return pl.pallas_call(
        _ple_kernel,
        grid=(t // tm,),
        in_specs=[row, pl.BlockSpec((tm, PLE_DIM), lambda i: (i, 0)), _const_spec((1, D_MODEL)),
                  _const_spec((D_MODEL, D_MODEL)), _const_spec((PLE_DIM, D_MODEL)),
                  _const_spec((1, D_MODEL)), _const_spec((1, D_MODEL))],
        out_specs=row,
        out_shape=jax.ShapeDtypeStruct((t, D_MODEL), F32),
        compiler_params=_params(1),
        name="ple",
    )(x, p, gn, wg, wp, pn, fn)


def kernel(x, p, ffn1_norm, ffn1_w1, ffn1_w3, ffn1_w2, mix_norm, w_in, hgrn_lb, hgrn_onorm, w_branch_a, pool_w, pool_scale, w_branch_b, w_out, ffn2_norm, ffn2_w1, ffn2_w3, ffn2_w2, ple_norm, ple_w_gate, ple_w_proj, ple_post_norm, final_norm):
    batch, seq, d = x.shape
    assert d == D_MODEL and p.shape[0] == 1 and hgrn_lb.shape == (2, HG_WIDTH)
    t = batch * seq
    bf = lambda w: w.astype(BF16)
    vec = lambda g: g.reshape(1, -1)

    xt = x.reshape(t, d)
    xt = _ffn(xt, vec(ffn1_norm[0]), bf(ffn1_w1[0]), bf(ffn1_w3[0]), bf(ffn1_w2[0]))
    q, k, v, lf, og, u, ga, gb = _proj(xt, vec(mix_norm[0]), bf(w_in[0]), hgrn_lb)
    o = _hgrn(q, k, v, lf, batch)
    xt = _mix(xt, o, og, u, ga, gb, vec(hgrn_onorm[0]), bf(w_branch_a[0]), bf(pool_w[0]),
              vec(pool_scale[0]), bf(w_branch_b[0]), bf(w_out[0]), batch)
    xt = _ffn(xt, vec(ffn2_norm[0]), bf(ffn2_w1[0]), bf(ffn2_w3[0]), bf(ffn2_w2[0]))
    out = _ple(xt, p[0].reshape(t, PLE_DIM), vec(ple_norm[0]), bf(ple_w_gate[0]), bf(ple_w_proj[0]),
               vec(ple_post_norm[0]), vec(final_norm))
    return out.reshape(batch, seq, d)
```

```python
import functools

import numpy as np
import jax
import jax.numpy as jnp
from jax import lax
from jax.experimental import pallas as pl
from jax.experimental.pallas import tpu as pltpu

D_MODEL = 1024
D_FF = 2816
PLE_DIM = 256
HEADS = 8
HEAD_DIM = 128
HG_WIDTH = HEADS * HEAD_DIM
POOL_WINDOWS = (2, 4, 8, 16)
POOL_CH = 128
POOL_WIDTH = len(POOL_WINDOWS) * POOL_CH
POOL_HALO = 16
IN_SPLITS = (0, 1024, 2048, 3072, 4096, 4608, 5632, 6656)
EPS = 1e-6

CHUNK = 64
LEVEL_HALVES = (1, 2, 4, 8, 16, 32)
N_LEVELS = len(LEVEL_HALVES) + 2

V7X_VMEM_BYTES = 64 * 1024 * 1024
VMEM_LIMIT = V7X_VMEM_BYTES - 8 * 1024 * 1024

BF16 = jnp.bfloat16
F32 = jnp.float32


def _rms(x, g):
    return x * lax.rsqrt(jnp.mean(x * x, axis=-1, keepdims=True) + EPS) * g


def _silu(x):
    return x * jax.nn.sigmoid(x)


def _dot(a, b):
    return jnp.dot(a, b, preferred_element_type=F32)


def _dot_nt(a, b):
    return lax.dot_general(a, b, (((1,), (1,)), ((), ())), preferred_element_type=F32)


def _const_spec(shape):
    zeros = (0,) * len(shape)
    return pl.BlockSpec(shape, lambda *_: zeros, pipeline_mode=pl.Buffered(1))


def _params(n_grid_dims, sequential=False):
    sem = ("arbitrary" if sequential else "parallel",) * n_grid_dims
    return pltpu.CompilerParams(dimension_semantics=sem, vmem_limit_bytes=VMEM_LIMIT)


def _ffn_kernel(x_ref, g_ref, w1_ref, w3_ref, w2_ref, o_ref):
    x = x_ref[...]
    h = _rms(x, g_ref[...]).astype(BF16)
    a = _dot(h, w1_ref[...])
    b = _dot(h, w3_ref[...])
    act = (_silu(a) * b).astype(BF16)
    o_ref[...] = x + 0.5 * _dot(act, w2_ref[...])


def _ffn(x, g, w1, w3, w2, tm=512):
    t = x.shape[0]
    row = pl.BlockSpec((tm, D_MODEL), lambda i: (i, 0))
    return pl.pallas_call(
        _ffn_kernel,
        grid=(t // tm,),
        in_specs=[row, _const_spec((1, D_MODEL)), _const_spec((D_MODEL, D_FF)),
                  _const_spec((D_MODEL, D_FF)), _const_spec((D_FF, D_MODEL))],
        out_specs=row,
        out_shape=jax.ShapeDtypeStruct((t, D_MODEL), F32),
        compiler_params=_params(1),
        name="ffn",
    )(x, g, w1, w3, w2)


def _proj_kernel(x_ref, g_ref, w_ref, lb_ref, q_ref, k_ref, v_ref, lf_ref, og_ref, u_ref, ga_ref, gb_ref):
    h = _rms(x_ref[...], g_ref[...]).astype(BF16)

    def seg(i):
        return _dot(h, w_ref[:, IN_SPLITS[i]:IN_SPLITS[i + 1]])

    q_ref[...] = _silu(seg(0))
    lbp = lb_ref[...]
    e = jnp.exp(lbp - jnp.max(lbp, axis=0, keepdims=True))
    lb = e[0:1, :] / jnp.sum(e, axis=0, keepdims=True)
    f = lb + (1.0 - lb) * jax.nn.sigmoid(seg(1))
    lf_ref[...] = jnp.log(f)
    k_ref[...] = 1.0 - f
    v_ref[...] = seg(2)
    og_ref[...] = _silu(seg(3))
    u_ref[...] = seg(4)
    ga_ref[...] = jax.nn.sigmoid(seg(5))
    gb_ref[...] = jax.nn.sigmoid(seg(6))


def _proj(x, g, w_in, hgrn_lb, tm=256):
    t = x.shape[0]
    row = pl.BlockSpec((tm, D_MODEL), lambda i: (i, 0))
    row_u = pl.BlockSpec((tm, POOL_WIDTH), lambda i: (i, 0))
    wide = jax.ShapeDtypeStruct((t, D_MODEL), F32)
    return pl.pallas_call(
        _proj_kernel,
        grid=(t // tm,),
        in_specs=[row, _const_spec((1, D_MODEL)), _const_spec((D_MODEL, IN_SPLITS[-1])),
                  _const_spec(hgrn_lb.shape)],
        out_specs=[row, row, row, row, row, row_u, row, row],
        out_shape=[wide, wide, wide, wide, wide, jax.ShapeDtypeStruct((t, POOL_WIDTH), F32), wide, wide],
        compiler_params=_params(1),
        name="proj",
    )(x, g, w_in, hgrn_lb)


def _level_tables():
    c = CHUNK
    sums = np.zeros((N_LEVELS, c, c), np.float32)
    masks = np.zeros((len(LEVEL_HALVES) + 1, c, c), np.float32)
    for li, b in enumerate(LEVEL_HALVES):
        for t in range(c):
            base = (t // (2 * b)) * 2 * b
            r = base + b - 1
            if t > r:
                sums[li, t, r + 1:t + 1] = 1.0
                masks[li, t, base:r + 1] = 1.0
            else:
                sums[li, t, t + 1:r + 1] = 1.0
    for t in range(c):
        sums[N_LEVELS - 2, t, :t + 1] = 1.0
        sums[N_LEVELS - 1, t, t + 1:] = 1.0
    masks[len(LEVEL_HALVES)] = np.eye(c, dtype=np.float32)
    return sums.reshape(N_LEVELS * c, c), masks


def _hgrn_kernel(q_ref, k_ref, v_ref, lf_ref, sums_ref, masks_ref, o_ref, st_ref, *, n_chunks):
    @pl.when(pl.program_id(1) == 0)
    def _():
        st_ref[...] = jnp.zeros_like(st_ref)

    row = lax.broadcasted_iota(jnp.int32, (CHUNK, 1), 0)

    head_cols = [slice(h * HEAD_DIM, (h + 1) * HEAD_DIM) for h in range(HEADS)]

    def intra(c):
        rows = slice(c * CHUNK, (c + 1) * CHUNK)
        q = q_ref[rows, :]
        k = k_ref[rows, :]
        v = v_ref[rows, :]
        lf = lf_ref[rows, :]
        l1 = lf.astype(BF16)
        rem = lf - l1.astype(F32)
        l2 = rem.astype(BF16)
        l3 = (rem - l2.astype(F32)).astype(BF16)
        expo = _dot(sums_ref[...], jnp.concatenate([l1, l2, l3], axis=0))
        decay = jnp.exp(expo)

        def level(i):
            return decay[i * CHUNK:(i + 1) * CHUNK, :]

        qb = q.astype(BF16)
        kb = k.astype(BF16)
        m = masks_ref[len(LEVEL_HALVES)]
        scores = [m * _dot_nt(qb[:, hc], kb[:, hc]) for hc in head_cols]
        for li, b in enumerate(LEVEL_HALVES):
            upper = (row % (2 * b)) >= b
            x = (jnp.where(upper, q, k) * level(li)).astype(BF16)
            m = masks_ref[li]
            for h, hc in enumerate(head_cols):
                scores[h] = scores[h] + m * _dot_nt(x[:, hc], x[:, hc])
        q_state = (q * level(N_LEVELS - 2)).astype(BF16)
        k_end = (k * level(N_LEVELS - 1)).astype(BF16)
        chunk_decay = decay[(N_LEVELS - 1) * CHUNK - 1:(N_LEVELS - 1) * CHUNK, :]
        o_intra = [_dot(scores[h].astype(BF16), v[:, hc].astype(BF16)) for h, hc in enumerate(head_cols)]
        v_t = [v[:, hc].T.astype(BF16) for hc in head_cols]
        return o_intra, q_state, k_end, chunk_decay, v_t

    def inter(c, o_intra, q_state, k_end, chunk_decay, v_t):
        rows = slice(c * CHUNK, (c + 1) * CHUNK)
        states = [st_ref[h] for h in range(HEADS)]
        for h, hc in enumerate(head_cols):
            o_ref[rows, hc] = o_intra[h] + _dot_nt(q_state[:, hc], states[h].astype(BF16))
        for h, hc in enumerate(head_cols):
            st_ref[h] = chunk_decay[:, hc] * states[h] + _dot(v_t[h], k_end[:, hc])

    pending = intra(0)
    for c in range(n_chunks):
        nxt = intra(c + 1) if c + 1 < n_chunks else None
        inter(c, *pending)
        pending = nxt


def _hgrn(q, k, v, lf, batch, tt=256):
    t = q.shape[0]
    steps = t // batch // tt
    sums, masks = _level_tables()
    sums3 = jnp.asarray(np.concatenate([sums, sums, sums], axis=1), BF16)
    row = pl.BlockSpec((tt, HG_WIDTH), lambda b, s: (b * steps + s, 0))
    return pl.pallas_call(
        functools.partial(_hgrn_kernel, n_chunks=tt // CHUNK),
        grid=(batch, steps),
        in_specs=[row, row, row, row, _const_spec(sums3.shape), _const_spec(masks.shape)],
        out_specs=row,
        out_shape=jax.ShapeDtypeStruct((t, HG_WIDTH), F32),
        scratch_shapes=[pltpu.VMEM((HEADS, HEAD_DIM, HEAD_DIM), F32)],
        compiler_params=_params(2, sequential=True),
        name="hgrn",
    )(q, k, v, lf, sums3, jnp.asarray(masks))


def _mix_kernel(x_ref, o_ref, og_ref, u_ref, uprev_ref, ga_ref, gb_ref, onorm_ref, wa_ref, pw_ref, ps_ref,
                wb_ref, wo_ref, out_ref, ext_ref, *, tm):
    s = pl.program_id(1)
    o = o_ref[...]
    heads = []
    for h in range(HEADS):
        oh = o[:, h * HEAD_DIM:(h + 1) * HEAD_DIM]
        heads.append(oh * lax.rsqrt(jnp.mean(oh * oh, axis=-1, keepdims=True) + EPS))
    on = jnp.concatenate(heads, axis=-1) * onorm_ref[...] * og_ref[...]
    ya = _dot(on.astype(BF16), wa_ref[...])

    ext_ref[0:POOL_HALO, :] = jnp.where(s == 0, 0.0, uprev_ref[...])
    ext_ref[POOL_HALO:POOL_HALO + tm, :] = u_ref[...]
    pos = (s * tm + 1 + lax.broadcasted_iota(jnp.int32, (tm, POOL_CH), 0)).astype(F32)
    groups = []
    for g, w in enumerate(POOL_WINDOWS):
        cols = slice(g * POOL_CH, (g + 1) * POOL_CH)
        tok = ext_ref[POOL_HALO:POOL_HALO + tm, cols]
        win = tok
        for j in range(1, w):
            win = win + ext_ref[POOL_HALO - j:POOL_HALO - j + tm, cols]
        pooled = win / jnp.minimum(pos, float(w)) - tok
        groups.append(_dot(pooled.astype(BF16), pw_ref[g]))
    mixed = jnp.concatenate(groups, axis=-1) * ps_ref[...]
    yb = _dot(mixed.astype(BF16), wb_ref[...])

    y = ga_ref[...] * ya + gb_ref[...] * yb
    out_ref[...] = x_ref[...] + _dot(y.astype(BF16), wo_ref[...])


def _mix(x, o, og, u, ga, gb, onorm, wa, pw, ps, wb, wo, batch, tm=256):
    t = x.shape[0]
    steps = t // batch // tm
    halo_blocks = tm // POOL_HALO
    row = pl.BlockSpec((tm, D_MODEL), lambda b, s: (b * steps + s, 0))
    row_u = pl.BlockSpec((tm, POOL_WIDTH), lambda b, s: (b * steps + s, 0))
    prev_u = pl.BlockSpec((POOL_HALO, POOL_WIDTH),
                          lambda b, s: (jnp.maximum((b * steps + s) * halo_blocks - 1, 0), 0))
    return pl.pallas_call(
        functools.partial(_mix_kernel, tm=tm),
        grid=(batch, steps),
        in_specs=[row, row, row, row_u, prev_u, row, row,
                  _const_spec((1, HG_WIDTH)), _const_spec((HG_WIDTH, D_MODEL)),
                  _const_spec((len(POOL_WINDOWS), POOL_CH, POOL_CH)), _const_spec((1, POOL_WIDTH)),
                  _const_spec((POOL_WIDTH, D_MODEL)), _const_spec((D_MODEL, D_MODEL))],
        out_specs=row,
        out_shape=jax.ShapeDtypeStruct((t, D_MODEL), F32),
        scratch_shapes=[pltpu.VMEM((POOL_HALO + tm, POOL_WIDTH), F32)],
        compiler_params=_params(2),
        name="mix",
    )(x, o, og, u, u, ga, gb, onorm, wa, pw, ps, wb, wo)


def _ple_kernel(x_ref, p_ref, gn_ref, wg_ref, wp_ref, pn_ref, fn_ref, o_ref):
    x = x_ref[...]
    gate = jax.nn.sigmoid(_dot(_rms(x, gn_ref[...]).astype(BF16), wg_ref[...]))
    e = _rms(_dot(p_ref[...].astype(BF16), wp_ref[...]), pn_ref[...])
    o_ref[...] = _rms(x + gate * e, fn_ref[...])


def _ple(x, p, gn, wg, wp, pn, fn, tm=512):
    t = x.shape[0]
    row = pl.BlockSpec((tm, D_MODEL), lambda i: (i, 0))
    return pl.pallas_call(
        _ple_kernel,
        grid=(t // tm,),
        in_specs=[row, pl.BlockSpec((tm, PLE_DIM), lambda i: (i, 0)), _const_spec((1, D_MODEL)),
                  _const_spec((D_MODEL, D_MODEL)), _const_spec((PLE_DIM, D_MODEL)),
                  _const_spec((1, D_MODEL)), _const_spec((1, D_MODEL))],
        out_specs=row,
        out_shape=jax.ShapeDtypeStruct((t, D_MODEL), F32),
        compiler_params=_params(1),
        name="ple",
    )(x, p, gn, wg, wp, pn, fn)


def kernel(x, p, ffn1_norm, ffn1_w1, ffn1_w3, ffn1_w2, mix_norm, w_in, hgrn_lb, hgrn_onorm, w_branch_a, pool_w, pool_scale, w_branch_b, w_out, ffn2_norm, ffn2_w1, ffn2_w3, ffn2_w2, ple_norm, ple_w_gate, ple_w_proj, ple_post_norm, final_norm):
    batch, seq, d = x.shape
    assert d == D_MODEL and p.shape[0] == 1 and hgrn_lb.shape == (2, HG_WIDTH)
    t = batch * seq
    bf = lambda w: w.astype(BF16)
    vec = lambda g: g.reshape(1, -1)

    xt = x.reshape(t, d)
    xt = _ffn(xt, vec(ffn1_norm[0]), bf(ffn1_w1[0]), bf(ffn1_w3[0]), bf(ffn1_w2[0]))
    q, k, v, lf, og, u, ga, gb = _proj(xt, vec(mix_norm[0]), bf(w_in[0]), hgrn_lb)
    o = _hgrn(q, k, v, lf, batch)
    xt = _mix(xt, o, og, u, ga, gb, vec(hgrn_onorm[0]), bf(w_branch_a[0]), bf(pool_w[0]),
              vec(pool_scale[0]), bf(w_branch_b[0]), bf(w_out[0]), batch)
    xt = _ffn(xt, vec(ffn2_norm[0]), bf(ffn2_w1[0]), bf(ffn2_w3[0]), bf(ffn2_w2[0]))
    out = _ple(xt, p[0].reshape(t, PLE_DIM), vec(ple_norm[0]), bf(ple_w_gate[0]), bf(ple_w_proj[0]),
               vec(ple_post_norm[0]), vec(final_norm))
    return out.reshape(batch, seq, d)
```

```python
import functools

import numpy as np
import jax
import jax.numpy as jnp
from jax import lax
from jax.experimental import pallas as pl
from jax.experimental.pallas import tpu as pltpu

D_MODEL = 1024
D_FF = 2816
PLE_DIM = 256
HEADS = 8
HEAD_DIM = 128
HG_WIDTH = HEADS * HEAD_DIM
POOL_WINDOWS = (2, 4, 8, 16)
POOL_CH = 128
POOL_WIDTH = len(POOL_WINDOWS) * POOL_CH
POOL_HALO = 16
IN_SPLITS = (0, 1024, 2048, 3072, 4096, 4608, 5632, 6656)
EPS = 1e-6
LOG2E = 1.4426950408889634

CHUNK = 64
FINE_HALVES = (2, 4)
LEVEL_HALVES = FINE_HALVES + (8, 16, 32)
SLAB = 16

V7X_VMEM_BYTES = 64 * 1024 * 1024
VMEM_LIMIT = V7X_VMEM_BYTES - 8 * 1024 * 1024

BF16 = jnp.bfloat16
F32 = jnp.float32


def _rms(x, g):
    return x * lax.rsqrt(jnp.mean(x * x, axis=-1, keepdims=True) + EPS) * g


def _silu(x):
    return x * jax.nn.sigmoid(x)


def _dot(a, b):
    return jnp.dot(a, b, preferred_element_type=F32)


def _dot_nt(a, b):
    return lax.dot_general(a, b, (((1,), (1,)), ((), ())), preferred_element_type=F32)


def _const_spec(shape):
    zeros = (0,) * len(shape)
    return pl.BlockSpec(shape, lambda *_: zeros, pipeline_mode=pl.Buffered(1))


def _params(n_grid_dims, sequential=False):
    sem = ("arbitrary" if sequential else "parallel",) * n_grid_dims
    return pltpu.CompilerParams(dimension_semantics=sem, vmem_limit_bytes=VMEM_LIMIT)


def _ffn_kernel(x_ref, g_ref, w1_ref, w3_ref, w2_ref, o_ref):
    x = x_ref[...]
    h = _rms(x, g_ref[...]).astype(BF16)
    a = _dot(h, w1_ref[...])
    b = _dot(h, w3_ref[...])
    act = (_silu(a) * b).astype(BF16)
    o_ref[...] = x + 0.5 * _dot(act, w2_ref[...])


def _ffn(x, g, w1, w3, w2, tm=512):
    t = x.shape[0]
    row = pl.BlockSpec((tm, D_MODEL), lambda i: (i, 0))
    return pl.pallas_call(
        _ffn_kernel,
        grid=(t // tm,),
        in_specs=[row, _const_spec((1, D_MODEL)), _const_spec((D_MODEL, D_FF)),
                  _const_spec((D_MODEL, D_FF)), _const_spec((D_FF, D_MODEL))],
        out_specs=row,
        out_shape=jax.ShapeDtypeStruct((t, D_MODEL), F32),
        compiler_params=_params(1),
        name="ffn",
    )(x, g, w1, w3, w2)


def _proj_kernel(x_ref, g_ref, w_ref, lb_ref, q_ref, k_ref, v_ref, lfh_ref, lfl_ref, og_ref, u_ref, ga_ref, gb_ref):
    h = _rms(x_ref[...], g_ref[...]).astype(BF16)

    def seg(i):
        return _dot(h, w_ref[:, IN_SPLITS[i]:IN_SPLITS[i + 1]])

    q_ref[...] = _silu(seg(0))
    lbp = lb_ref[...]
    e = jnp.exp(lbp - jnp.max(lbp, axis=0, keepdims=True))
    lb = e[0:1, :] / jnp.sum(e, axis=0, keepdims=True)
    f = lb + (1.0 - lb) * jax.nn.sigmoid(seg(1))
    lf = jnp.log(f) * LOG2E
    hi = lf.astype(BF16)
    lfh_ref[...] = hi
    lfl_ref[...] = (lf - hi.astype(F32)).astype(BF16)
    k_ref[...] = 1.0 - f
    v_ref[...] = seg(2)
    og_ref[...] = _silu(seg(3))
    u_ref[...] = seg(4)
    ga_ref[...] = jax.nn.sigmoid(seg(5))
    gb_ref[...] = jax.nn.sigmoid(seg(6))


def _proj(x, g, w_in, hgrn_lb, tm=256):
    t = x.shape[0]
    row = pl.BlockSpec((tm, D_MODEL), lambda i: (i, 0))
    row_u = pl.BlockSpec((tm, POOL_WIDTH), lambda i: (i, 0))
    wide = jax.ShapeDtypeStruct((t, D_MODEL), F32)
    half = jax.ShapeDtypeStruct((t, D_MODEL), BF16)
    return pl.pallas_call(
        _proj_kernel,
        grid=(t // tm,),
        in_specs=[row, _const_spec((1, D_MODEL)), _const_spec((D_MODEL, IN_SPLITS[-1])),
                  _const_spec(hgrn_lb.shape)],
        out_specs=[row, row, row, row, row, row, row_u, row, row],
        out_shape=[wide, wide, wide, half, half, wide, jax.ShapeDtypeStruct((t, POOL_WIDTH), F32), wide, wide],
        compiler_params=_params(1),
        name="proj",
    )(x, g, w_in, hgrn_lb)


def _level_tables():
    c = CHUNK
    sums = np.zeros((len(FINE_HALVES) + 1, c, c), np.float32)
    masks = np.zeros((len(LEVEL_HALVES), c, c), np.float32)
    for li, b in enumerate(LEVEL_HALVES):
        for t in range(c):
            base = (t // (2 * b)) * 2 * b
            r = base + b - 1
            if t > r:
                masks[li, t, base:r + 1] = 1.0
            if b in FINE_HALVES:
                sums[li, t, (r + 1 if t > r else t + 1):(t + 1 if t > r else r + 1)] = 1.0
    for t in range(c):
        sums[len(FINE_HALVES), t, :t + 1] = 1.0
    return sums.reshape(-1, c), masks


def _hgrn_kernel(q_ref, k_ref, v_ref, lfh_ref, lfl_ref, sums_ref, masks_ref, o_ref, st_ref, *, n_chunks):
    @pl.when(pl.program_id(1) == 0)
    def _():
        st_ref[...] = jnp.zeros_like(st_ref)

    row = lax.broadcasted_iota(jnp.int32, (CHUNK, 1), 0)
    odd = (row % 2) == 1
    head_cols = [slice(h * HEAD_DIM, (h + 1) * HEAD_DIM) for h in range(HEADS)]
    n_slabs = CHUNK // SLAB

    def head_sums(x):
        return [jnp.sum(x[:, hc], axis=-1, keepdims=True) for hc in head_cols]

    def intra(c):
        rows = slice(c * CHUNK, (c + 1) * CHUNK)
        q = q_ref[rows, :]
        k = k_ref[rows, :]
        v = v_ref[rows, :]
        expo = _dot(sums_ref[...], jnp.concatenate([lfh_ref[rows, :], lfl_ref[rows, :]], axis=0))
        g = expo[len(FINE_HALVES) * CHUNK:, :]

        diag = head_sums(q * k)
        pair = head_sums(jnp.where(odd, q * (1.0 - k) * pltpu.roll(k, 1, 0), 0.0))
        v_prev = pltpu.roll(v, 1, 0)

        slabs = [[None] * n_slabs for _ in range(HEADS)]

        def add(h, j, val):
            slabs[h][j] = val if slabs[h][j] is None else slabs[h][j] + val

        for li, b in enumerate(LEVEL_HALVES):
            m = masks_ref[li]
            if b in FINE_HALVES:
                qk = jnp.where((row % (2 * b)) >= b, q, k)
                x = (qk * jnp.exp2(expo[li * CHUNK:(li + 1) * CHUNK, :])).astype(BF16)
            else:
                pieces = []
                for r0 in range(0, CHUNK, b):
                    r = (r0 // (2 * b)) * 2 * b + b - 1
                    if (r0 // b) % 2:
                        pieces.append(q[r0:r0 + b] * jnp.exp2(g[r0:r0 + b] - g[r:r + 1]))
                    else:
                        pieces.append(k[r0:r0 + b] * jnp.exp2(g[r:r + 1] - g[r0:r0 + b]))
                x = jnp.concatenate(pieces, axis=0).astype(BF16)
            if b % SLAB == 0:
                upper = [j for j in range(n_slabs) if (j * SLAB // b) % 2]
                lhs = jnp.concatenate([x[j * SLAB:(j + 1) * SLAB] for j in upper], axis=0)
                ml = jnp.concatenate([m[j * SLAB:(j + 1) * SLAB] for j in upper], axis=0)
            else:
                upper, lhs, ml = list(range(n_slabs)), x, m
            for h, hc in enumerate(head_cols):
                s = ml * _dot_nt(lhs[:, hc], x[:, hc])
                for i, j in enumerate(upper):
                    add(h, j, s[i * SLAB:(i + 1) * SLAB])
        q_state = (q * jnp.exp2(g)).astype(BF16)
        k_end = k * jnp.exp2(g[CHUNK - 1:CHUNK] - g)
        chunk_decay = jnp.exp2(g[CHUNK - 1:CHUNK])
        o_intra, k_end_t, decay_col = [], [], []
        for h, hc in enumerate(head_cols):
            scores = jnp.concatenate(slabs[h], axis=0)
            o_intra.append(_dot(scores.astype(BF16), v[:, hc].astype(BF16))
                           + diag[h] * v[:, hc] + pair[h] * v_prev[:, hc])
            k_end_t.append(k_end[:, hc].T.astype(BF16))
            decay_col.append(jnp.broadcast_to(chunk_decay[:, hc], (8, HEAD_DIM)).T[:, 0:1])
        return o_intra, q_state, k_end_t, decay_col, v.astype(BF16)

    def inter(c, o_intra, q_state, k_end_t, decay_col, vb):
        rows = slice(c * CHUNK, (c + 1) * CHUNK)
        states = [st_ref[h] for h in range(HEADS)]
        for h, hc in enumerate(head_cols):
            o_ref[rows, hc] = o_intra[h] + _dot(q_state[:, hc], states[h].astype(BF16))
        for h, hc in enumerate(head_cols):
            st_ref[h] = decay_col[h] * states[h] + _dot(k_end_t[h], vb[:, hc])

    pending = intra(0)
    for c in range(n_chunks):
        nxt = intra(c + 1) if c + 1 < n_chunks else None
        inter(c, *pending)
        pending = nxt


def _hgrn(q, k, v, lfh, lfl, batch, tt=256):
    t = q.shape[0]
    steps = t // batch // tt
    sums, masks = _level_tables()
    sums2 = jnp.asarray(np.concatenate([sums, sums], axis=1), BF16)
    row = pl.BlockSpec((tt, HG_WIDTH), lambda b, s: (b * steps + s, 0))
    return pl.pallas_call(
        functools.partial(_hgrn_kernel, n_chunks=tt // CHUNK),
        grid=(batch, steps),
        in_specs=[row, row, row, row, row, _const_spec(sums2.shape), _const_spec(masks.shape)],
        out_specs=row,
        out_shape=jax.ShapeDtypeStruct((t, HG_WIDTH), F32),
        scratch_shapes=[pltpu.VMEM((HEADS, HEAD_DIM, HEAD_DIM), F32)],
        compiler_params=_params(2, sequential=True),
        name="hgrn",
    )(q, k, v, lfh, lfl, sums2, jnp.asarray(masks))


def _mix_kernel(x_ref, o_ref, og_ref, u_ref, uprev_ref, ga_ref, gb_ref, onorm_ref, wa_ref, pw_ref, ps_ref,
                wb_ref, wo_ref, out_ref, ext_ref, *, tm):
    s = pl.program_id(1)
    o = o_ref[...]
    heads = []
    for h in range(HEADS):
        oh = o[:, h * HEAD_DIM:(h + 1) * HEAD_DIM]
        heads.append(oh * lax.rsqrt(jnp.mean(oh * oh, axis=-1, keepdims=True) + EPS))
    on = jnp.concatenate(heads, axis=-1) * onorm_ref[...] * og_ref[...]
    ya = _dot(on.astype(BF16), wa_ref[...])

    ext_ref[0:POOL_HALO, :] = jnp.where(s == 0, 0.0, uprev_ref[...])
    ext_ref[POOL_HALO:POOL_HALO + tm, :] = u_ref[...]
    pos = (s * tm + 1 + lax.broadcasted_iota(jnp.int32, (tm, POOL_CH), 0)).astype(F32)
    groups = []
    for g, w in enumerate(POOL_WINDOWS):
        cols = slice(g * POOL_CH, (g + 1) * POOL_CH)
        tok = ext_ref[POOL_HALO:POOL_HALO + tm, cols]
        win = tok
        for j in range(1, w):
            win = win + ext_ref[POOL_HALO - j:POOL_HALO - j + tm, cols]
        pooled = win / jnp.minimum(pos, float(w)) - tok
        groups.append(_dot(pooled.astype(BF16), pw_ref[g]))
    mixed = jnp.concatenate(groups, axis=-1) * ps_ref[...]
    yb = _dot(mixed.astype(BF16), wb_ref[...])

    y = ga_ref[...] * ya + gb_ref[...] * yb
    out_ref[...] = x_ref[...] + _dot(y.astype(BF16), wo_ref[...])


def _mix(x, o, og, u, ga, gb, onorm, wa, pw, ps, wb, wo, batch, tm=256):
    t = x.shape[0]
    steps = t // batch // tm
    halo_blocks = tm // POOL_HALO
    row = pl.BlockSpec((tm, D_MODEL), lambda b, s: (b * steps + s, 0))
    row_u = pl.BlockSpec((tm, POOL_WIDTH), lambda b, s: (b * steps + s, 0))
    prev_u = pl.BlockSpec((POOL_HALO, POOL_WIDTH),
                          lambda b, s: (jnp.maximum((b * steps + s) * halo_blocks - 1, 0), 0))
    return pl.pallas_call(
        functools.partial(_mix_kernel, tm=tm),
        grid=(batch, steps),
        in_specs=[row, row, row, row_u, prev_u, row, row,
                  _const_spec((1, HG_WIDTH)), _const_spec((HG_WIDTH, D_MODEL)),
                  _const_spec((len(POOL_WINDOWS), POOL_CH, POOL_CH)), _const_spec((1, POOL_WIDTH)),
                  _const_spec((POOL_WIDTH, D_MODEL)), _const_spec((D_MODEL, D_MODEL))],
        out_specs=row,
        out_shape=jax.ShapeDtypeStruct((t, D_MODEL), F32),
        scratch_shapes=[pltpu.VMEM((POOL_HALO + tm, POOL_WIDTH), F32)],
        compiler_params=_params(2),
        name="mix",
    )(x, o, og, u, u, ga, gb, onorm, wa, pw, ps, wb, wo)


def _ple_kernel(x_ref, p_ref, gn_ref, wg_ref, wp_ref, pn_ref, fn_ref, o_ref):
    x = x_ref[...]
    gate = jax.nn.sigmoid(_dot(_rms(x, gn_ref[...]).astype(BF16), wg_ref[...]))
    e = _rms(_dot(p_ref[...].astype(BF16), wp_ref[...]), pn_ref[...])
    o_ref[...] = _rms(x + gate * e, fn_ref[...])


def _ple(x, p, gn, wg, wp, pn, fn, tm=512):
    t = x.shape[0]
    row = pl.BlockSpec((tm, D_MODEL), lambda i: (i, 0))
    return pl.pallas_call(
        _ple_kernel,
        grid=(t // tm,),
        in_specs=[row, pl.BlockSpec((tm, PLE_DIM), lambda i: (i, 0)), _const_spec((1, D_MODEL)),
                  _const_spec((D_MODEL, D_MODEL)), _const_spec((PLE_DIM, D_MODEL)),
                  _const_spec((1, D_MODEL)), _const_spec((1, D_MODEL))],
        out_specs=row,
        out_shape=jax.ShapeDtypeStruct((t, D_MODEL), F32),
        compiler_params=_params(1),
        name="ple",
    )(x, p, gn, wg, wp, pn, fn)


def kernel(x, p, ffn1_norm, ffn1_w1, ffn1_w3, ffn1_w2, mix_norm, w_in, hgrn_lb, hgrn_onorm, w_branch_a, pool_w, pool_scale, w_branch_b, w_out, ffn2_norm, ffn2_w1, ffn2_w3, ffn2_w2, ple_norm, ple_w_gate, ple_w_proj, ple_post_norm, final_norm):
    batch, seq, d = x.shape
    assert d == D_MODEL and p.shape[0] == 1 and hgrn_lb.shape == (2, HG_WIDTH)
    t = batch * seq
    bf = lambda w: w.astype(BF16)
    vec = lambda g: g.reshape(1, -1)

    xt = x.reshape(t, d)
    xt = _ffn(xt, vec(ffn1_norm[0]), bf(ffn1_w1[0]), bf(ffn1_w3[0]), bf(ffn1_w2[0]))
    q, k, v, lfh, lfl, og, u, ga, gb = _proj(xt, vec(mix_norm[0]), bf(w_in[0]), hgrn_lb)
    o = _hgrn(q, k, v, lfh, lfl, batch)
    xt = _mix(xt, o, og, u, ga, gb, vec(hgrn_onorm[0]), bf(w_branch_a[0]), bf(pool_w[0]),
              vec(pool_scale[0]), bf(w_branch_b[0]), bf(w_out[0]), batch)
    xt = _ffn(xt, vec(ffn2_norm[0]), bf(ffn2_w1[0]), bf(ffn2_w3[0]), bf(ffn2_w2[0]))
    out = _ple(xt, p[0].reshape(t, PLE_DIM), vec(ple_norm[0]), bf(ple_w_gate[0]), bf(ple_w_proj[0]),
               vec(ple_post_norm[0]), vec(final_norm))
    return out.reshape(batch, seq, d)
```

```python
import functools

import numpy as np
import jax
import jax.numpy as jnp
from jax import lax
from jax.experimental import pallas as pl
from jax.experimental.pallas import tpu as pltpu

D_MODEL = 1024
D_FF = 2816
PLE_DIM = 256
HEADS = 8
HEAD_DIM = 128
HG_WIDTH = HEADS * HEAD_DIM
POOL_WINDOWS = (2, 4, 8, 16)
POOL_CH = 128
POOL_WIDTH = len(POOL_WINDOWS) * POOL_CH
POOL_HALO = 16
IN_SPLITS = (0, 1024, 2048, 3072, 4096, 4608, 5632, 6656)
EPS = 1e-6
LOG2E = 1.4426950408889634

CHUNK = 64
FINE_HALVES = (2, 4)
LEVEL_HALVES = FINE_HALVES + (8, 16, 32)
SLAB = 16

V7X_VMEM_BYTES = 64 * 1024 * 1024
VMEM_LIMIT = V7X_VMEM_BYTES - 8 * 1024 * 1024

BF16 = jnp.bfloat16
F32 = jnp.float32


def _rms(x, g):
    return x * lax.rsqrt(jnp.mean(x * x, axis=-1, keepdims=True) + EPS) * g


def _silu(x):
    return x * jax.nn.sigmoid(x)


def _dot(a, b):
    return jnp.dot(a, b, preferred_element_type=F32)


def _dot_nt(a, b):
    return lax.dot_general(a, b, (((1,), (1,)), ((), ())), preferred_element_type=F32)


def _const_spec(shape):
    zeros = (0,) * len(shape)
    return pl.BlockSpec(shape, lambda *_: zeros, pipeline_mode=pl.Buffered(1))


def _params(n_grid_dims, sequential=False):
    sem = ("arbitrary" if sequential else "parallel",) * n_grid_dims
    return pltpu.CompilerParams(dimension_semantics=sem, vmem_limit_bytes=VMEM_LIMIT)


def _ffn_kernel(x_ref, g_ref, w1_ref, w3_ref, w2_ref, o_ref):
    x = x_ref[...]
    h = _rms(x, g_ref[...]).astype(BF16)
    a = _dot(h, w1_ref[...])
    b = _dot(h, w3_ref[...])
    act = (_silu(a) * b).astype(BF16)
    o_ref[...] = x + 0.5 * _dot(act, w2_ref[...])


def _ffn(x, g, w1, w3, w2, tm=512):
    t = x.shape[0]
    row = pl.BlockSpec((tm, D_MODEL), lambda i: (i, 0))
    return pl.pallas_call(
        _ffn_kernel,
        grid=(t // tm,),
        in_specs=[row, _const_spec((1, D_MODEL)), _const_spec((D_MODEL, D_FF)),
                  _const_spec((D_MODEL, D_FF)), _const_spec((D_FF, D_MODEL))],
        out_specs=row,
        out_shape=jax.ShapeDtypeStruct((t, D_MODEL), F32),
        compiler_params=_params(1),
        name="ffn",
    )(x, g, w1, w3, w2)


def _proj_kernel(x_ref, g_ref, w_ref, lb_ref, q_ref, k_ref, v_ref, lfh_ref, lfl_ref, og_ref, u_ref, ga_ref, gb_ref):
    h = _rms(x_ref[...], g_ref[...]).astype(BF16)

    def seg(i):
        return _dot(h, w_ref[:, IN_SPLITS[i]:IN_SPLITS[i + 1]])

    q_ref[...] = _silu(seg(0))
    lbp = lb_ref[...]
    e = jnp.exp(lbp - jnp.max(lbp, axis=0, keepdims=True))
    lb = e[0:1, :] / jnp.sum(e, axis=0, keepdims=True)
    f = lb + (1.0 - lb) * jax.nn.sigmoid(seg(1))
    lf = jnp.log(f) * LOG2E
    hi = lf.astype(BF16)
    lfh_ref[...] = hi
    lfl_ref[...] = (lf - hi.astype(F32)).astype(BF16)
    k_ref[...] = 1.0 - f
    v_ref[...] = seg(2).astype(BF16)
    og_ref[...] = _silu(seg(3)).astype(BF16)
    u_ref[...] = seg(4).astype(BF16)
    ga_ref[...] = jax.nn.sigmoid(seg(5)).astype(BF16)
    gb_ref[...] = jax.nn.sigmoid(seg(6)).astype(BF16)


def _proj(x, g, w_in, hgrn_lb, tm=256):
    t = x.shape[0]
    row = pl.BlockSpec((tm, D_MODEL), lambda i: (i, 0))
    row_u = pl.BlockSpec((tm, POOL_WIDTH), lambda i: (i, 0))
    wide = jax.ShapeDtypeStruct((t, D_MODEL), F32)
    half = jax.ShapeDtypeStruct((t, D_MODEL), BF16)
    return pl.pallas_call(
        _proj_kernel,
        grid=(t // tm,),
        in_specs=[row, _const_spec((1, D_MODEL)), _const_spec((D_MODEL, IN_SPLITS[-1])),
                  _const_spec(hgrn_lb.shape)],
        out_specs=[row, row, row, row, row, row, row_u, row, row],
        out_shape=[wide, wide, half, half, half, half, jax.ShapeDtypeStruct((t, POOL_WIDTH), BF16), half, half],
        compiler_params=_params(1),
        name="proj",
    )(x, g, w_in, hgrn_lb)


def _level_tables():
    c = CHUNK
    sums = np.zeros((len(FINE_HALVES) + 1, c, c), np.float32)
    masks = np.zeros((len(LEVEL_HALVES), c, c), np.float32)
    for li, b in enumerate(LEVEL_HALVES):
        for t in range(c):
            base = (t // (2 * b)) * 2 * b
            r = base + b - 1
            if t > r:
                masks[li, t, base:r + 1] = 1.0
            if b in FINE_HALVES:
                sums[li, t, (r + 1 if t > r else t + 1):(t + 1 if t > r else r + 1)] = 1.0
    for t in range(c):
        sums[len(FINE_HALVES), t, :t + 1] = 1.0
    return sums.reshape(-1, c), masks


def _hgrn_kernel(q_ref, k_ref, v_ref, lfh_ref, lfl_ref, sums_ref, masks_ref, o_ref, st_ref, *, n_chunks):
    @pl.when(pl.program_id(1) == 0)
    def _():
        st_ref[...] = jnp.zeros_like(st_ref)

    row = lax.broadcasted_iota(jnp.int32, (CHUNK, 1), 0)
    odd = (row % 2) == 1
    head_cols = [slice(h * HEAD_DIM, (h + 1) * HEAD_DIM) for h in range(HEADS)]
    n_slabs = CHUNK // SLAB

    def head_sums(x):
        return [jnp.sum(x[:, hc], axis=-1, keepdims=True) for hc in head_cols]

    def intra(c):
        rows = slice(c * CHUNK, (c + 1) * CHUNK)
        q = q_ref[rows, :]
        k = k_ref[rows, :]
        vb = v_ref[rows, :]
        v = vb.astype(F32)
        expo = _dot(sums_ref[...], jnp.concatenate([lfh_ref[rows, :], lfl_ref[rows, :]], axis=0))
        g = expo[len(FINE_HALVES) * CHUNK:, :]

        diag = head_sums(q * k)
        pair = head_sums(jnp.where(odd, q * (1.0 - k) * pltpu.roll(k, 1, 0), 0.0))
        v_prev = pltpu.roll(v, 1, 0)

        slabs = [[None] * n_slabs for _ in range(HEADS)]

        def add(h, j, val):
            slabs[h][j] = val if slabs[h][j] is None else slabs[h][j] + val

        for li, b in enumerate(LEVEL_HALVES):
            m = masks_ref[li]
            if b in FINE_HALVES:
                qk = jnp.where((row % (2 * b)) >= b, q, k)
                x = (qk * jnp.exp2(expo[li * CHUNK:(li + 1) * CHUNK, :])).astype(BF16)
            else:
                pieces = []
                for r0 in range(0, CHUNK, b):
                    r = (r0 // (2 * b)) * 2 * b + b - 1
                    if (r0 // b) % 2:
                        pieces.append(q[r0:r0 + b] * jnp.exp2(g[r0:r0 + b] - g[r:r + 1]))
                    else:
                        pieces.append(k[r0:r0 + b] * jnp.exp2(g[r:r + 1] - g[r0:r0 + b]))
                x = jnp.concatenate(pieces, axis=0).astype(BF16)
            if b % SLAB == 0:
                upper = [j for j in range(n_slabs) if (j * SLAB // b) % 2]
                lhs = jnp.concatenate([x[j * SLAB:(j + 1) * SLAB] for j in upper], axis=0)
                ml = jnp.concatenate([m[j * SLAB:(j + 1) * SLAB] for j in upper], axis=0)
            else:
                upper, lhs, ml = list(range(n_slabs)), x, m
            for h, hc in enumerate(head_cols):
                s = ml * _dot_nt(lhs[:, hc], x[:, hc])
                for i, j in enumerate(upper):
                    add(h, j, s[i * SLAB:(i + 1) * SLAB])
        q_state = (q * jnp.exp2(g)).astype(BF16)
        k_end = k * jnp.exp2(g[CHUNK - 1:CHUNK] - g)
        chunk_decay = jnp.exp2(g[CHUNK - 1:CHUNK])
        o_intra, k_end_t, decay_col = [], [], []
        for h, hc in enumerate(head_cols):
            scores = jnp.concatenate(slabs[h], axis=0)
            o_intra.append(_dot(scores.astype(BF16), vb[:, hc])
                           + diag[h] * v[:, hc] + pair[h] * v_prev[:, hc])
            k_end_t.append(k_end[:, hc].T.astype(BF16))
            decay_col.append(jnp.broadcast_to(chunk_decay[:, hc], (8, HEAD_DIM)).T[:, 0:1])
        return o_intra, q_state, k_end_t, decay_col, vb

    def inter(c, o_intra, q_state, k_end_t, decay_col, vb):
        rows = slice(c * CHUNK, (c + 1) * CHUNK)
        states = [st_ref[h] for h in range(HEADS)]
        for h, hc in enumerate(head_cols):
            o_ref[rows, hc] = (o_intra[h] + _dot(q_state[:, hc], states[h].astype(BF16))).astype(BF16)
        for h, hc in enumerate(head_cols):
            st_ref[h] = decay_col[h] * states[h] + _dot(k_end_t[h], vb[:, hc])

    pending = intra(0)
    for c in range(n_chunks):
        nxt = intra(c + 1) if c + 1 < n_chunks else None
        inter(c, *pending)
        pending = nxt


def _hgrn(q, k, v, lfh, lfl, batch, tt=256):
    t = q.shape[0]
    steps = t // batch // tt
    sums, masks = _level_tables()
    sums2 = jnp.asarray(np.concatenate([sums, sums], axis=1), BF16)
    row = pl.BlockSpec((tt, HG_WIDTH), lambda b, s: (b * steps + s, 0))
    return pl.pallas_call(
        functools.partial(_hgrn_kernel, n_chunks=tt // CHUNK),
        grid=(batch, steps),
        in_specs=[row, row, row, row, row, _const_spec(sums2.shape), _const_spec(masks.shape)],
        out_specs=row,
        out_shape=jax.ShapeDtypeStruct((t, HG_WIDTH), BF16),
        scratch_shapes=[pltpu.VMEM((HEADS, HEAD_DIM, HEAD_DIM), F32)],
        compiler_params=_params(2, sequential=True),
        name="hgrn",
    )(q, k, v, lfh, lfl, sums2, jnp.asarray(masks))


def _mix_kernel(x_ref, o_ref, og_ref, u_ref, uprev_ref, ga_ref, gb_ref, onorm_ref, wa_ref, pw_ref, ps_ref,
                wb_ref, wo_ref, out_ref, ext_ref, *, tm):
    s = pl.program_id(1)
    o = o_ref[...].astype(F32)
    heads = []
    for h in range(HEADS):
        oh = o[:, h * HEAD_DIM:(h + 1) * HEAD_DIM]
        heads.append(oh * lax.rsqrt(jnp.mean(oh * oh, axis=-1, keepdims=True) + EPS))
    on = jnp.concatenate(heads, axis=-1) * onorm_ref[...] * og_ref[...]
    ya = _dot(on.astype(BF16), wa_ref[...])

    ext_ref[0:POOL_HALO, :] = jnp.where(s == 0, 0.0, uprev_ref[...].astype(F32))
    ext_ref[POOL_HALO:POOL_HALO + tm, :] = u_ref[...].astype(F32)
    pos = (s * tm + 1 + lax.broadcasted_iota(jnp.int32, (tm, POOL_CH), 0)).astype(F32)
    groups = []
    for g, w in enumerate(POOL_WINDOWS):
        cols = slice(g * POOL_CH, (g + 1) * POOL_CH)
        tok = ext_ref[POOL_HALO:POOL_HALO + tm, cols]
        win = tok
        for j in range(1, w):
            win = win + ext_ref[POOL_HALO - j:POOL_HALO - j + tm, cols]
        pooled = win / jnp.minimum(pos, float(w)) - tok
        groups.append(_dot(pooled.astype(BF16), pw_ref[g]))
    mixed = jnp.concatenate(groups, axis=-1) * ps_ref[...]
    yb = _dot(mixed.astype(BF16), wb_ref[...])

    y = ga_ref[...] * ya + gb_ref[...] * yb
    out_ref[...] = x_ref[...] + _dot(y.astype(BF16), wo_ref[...])


def _mix(x, o, og, u, ga, gb, onorm, wa, pw, ps, wb, wo, batch, tm=256):
    t = x.shape[0]
    steps = t // batch // tm
    halo_blocks = tm // POOL_HALO
    row = pl.BlockSpec((tm, D_MODEL), lambda b, s: (b * steps + s, 0))
    row_u = pl.BlockSpec((tm, POOL_WIDTH), lambda b, s: (b * steps + s, 0))
    prev_u = pl.BlockSpec((POOL_HALO, POOL_WIDTH),
                          lambda b, s: (jnp.maximum((b * steps + s) * halo_blocks - 1, 0), 0))
    return pl.pallas_call(
        functools.partial(_mix_kernel, tm=tm),
        grid=(batch, steps),
        in_specs=[row, row, row, row_u, prev_u, row, row,
                  _const_spec((1, HG_WIDTH)), _const_spec((HG_WIDTH, D_MODEL)),
                  _const_spec((len(POOL_WINDOWS), POOL_CH, POOL_CH)), _const_spec((1, POOL_WIDTH)),
                  _const_spec((POOL_WIDTH, D_MODEL)), _const_spec((D_MODEL, D_MODEL))],
        out_specs=row,
        out_shape=jax.ShapeDtypeStruct((t, D_MODEL), F32),
        scratch_shapes=[pltpu.VMEM((POOL_HALO + tm, POOL_WIDTH), F32)],
        compiler_params=_params(2),
        name="mix",
    )(x, o, og, u, u, ga, gb, onorm, wa, pw, ps, wb, wo)


def _ffn_ple_kernel(x_ref, p_ref, g_ref, w1_ref, w3_ref, w2_ref, gn_ref, wg_ref, wp_ref, pn_ref, fn_ref, o_ref):
    x = x_ref[...]
    h = _rms(x, g_ref[...]).astype(BF16)
    act = (_silu(_dot(h, w1_ref[...])) * _dot(h, w3_ref[...])).astype(BF16)
    x = x + 0.5 * _dot(act, w2_ref[...])
    gate = jax.nn.sigmoid(_dot(_rms(x, gn_ref[...]).astype(BF16), wg_ref[...]))
    e = _rms(_dot(p_ref[...].astype(BF16), wp_ref[...]), pn_ref[...])
    o_ref[...] = _rms(x + gate * e, fn_ref[...])


def _ffn_ple(x, p, g, w1, w3, w2, gn, wg, wp, pn, fn, tm=512):
    t = x.shape[0]
    row = pl.BlockSpec((tm, D_MODEL), lambda i: (i, 0))
    vec = _const_spec((1, D_MODEL))
    return pl.pallas_call(
        _ffn_ple_kernel,
        grid=(t // tm,),
        in_specs=[row, pl.BlockSpec((tm, PLE_DIM), lambda i: (i, 0)), vec, _const_spec((D_MODEL, D_FF)),
                  _const_spec((D_MODEL, D_FF)), _const_spec((D_FF, D_MODEL)), vec,
                  _const_spec((D_MODEL, D_MODEL)), _const_spec((PLE_DIM, D_MODEL)), vec, vec],
        out_specs=row,
        out_shape=jax.ShapeDtypeStruct((t, D_MODEL), F32),
        compiler_params=_params(1),
        name="ffn_ple",
    )(x, p, g, w1, w3, w2, gn, wg, wp, pn, fn)


def kernel(x, p, ffn1_norm, ffn1_w1, ffn1_w3, ffn1_w2, mix_norm, w_in, hgrn_lb, hgrn_onorm, w_branch_a, pool_w, pool_scale, w_branch_b, w_out, ffn2_norm, ffn2_w1, ffn2_w3, ffn2_w2, ple_norm, ple_w_gate, ple_w_proj, ple_post_norm, final_norm):
    batch, seq, d = x.shape
    assert d == D_MODEL and p.shape[0] == 1 and hgrn_lb.shape == (2, HG_WIDTH)
    t = batch * seq
    bf = lambda w: w.astype(BF16)
    vec = lambda g: g.reshape(1, -1)

    xt = x.reshape(t, d)
    xt = _ffn(xt, vec(ffn1_norm[0]), bf(ffn1_w1[0]), bf(ffn1_w3[0]), bf(ffn1_w2[0]))
    q, k, v, lfh, lfl, og, u, ga, gb = _proj(xt, vec(mix_norm[0]), bf(w_in[0]), hgrn_lb)
    o = _hgrn(q, k, v, lfh, lfl, batch)
    xt = _mix(xt, o, og, u, ga, gb, vec(hgrn_onorm[0]), bf(w_branch_a[0]), bf(pool_w[0]),
              vec(pool_scale[0]), bf(w_branch_b[0]), bf(w_out[0]), batch)
    out = _ffn_ple(xt, p[0].reshape(t, PLE_DIM), vec(ffn2_norm[0]), bf(ffn2_w1[0]), bf(ffn2_w3[0]), bf(ffn2_w2[0]),
                   vec(ple_norm[0]), bf(ple_w_gate[0]), bf(ple_w_proj[0]), vec(ple_post_norm[0]), vec(final_norm))
    return out.reshape(batch, seq, d)
```

```python
import functools

import numpy as np
import jax
import jax.numpy as jnp
from jax import lax
from jax.experimental import pallas as pl
from jax.experimental.pallas import tpu as pltpu

D_MODEL = 1024
D_FF = 2816
PLE_DIM = 256
HEADS = 8
HEAD_DIM = 128
HG_WIDTH = HEADS * HEAD_DIM
POOL_WINDOWS = (2, 4, 8, 16)
POOL_CH = 128
POOL_WIDTH = len(POOL_WINDOWS) * POOL_CH
POOL_HALO = 16
IN_SPLITS = (0, 1024, 2048, 3072, 4096, 4608, 5632, 6656)
EPS = 1e-6
LOG2E = 1.4426950408889634

CHUNK = 64
FINE_HALVES = (2, 4)
LEVEL_HALVES = FINE_HALVES + (8, 16, 32)
SLAB = 16

TILE = 256
PAIR = 2 * TILE
PROJ_COLS = 512

V7X_VMEM_BYTES = 64 * 1024 * 1024
VMEM_LIMIT = V7X_VMEM_BYTES - 8 * 1024 * 1024

BF16 = jnp.bfloat16
F32 = jnp.float32


def _rms(x, g):
    return x * lax.rsqrt(jnp.mean(x * x, axis=-1, keepdims=True) + EPS) * g


def _silu(x):
    return x * jax.nn.sigmoid(x)


def _dot(a, b):
    return jnp.dot(a, b, preferred_element_type=F32)


def _dot_nt(a, b):
    return lax.dot_general(a, b, (((1,), (1,)), ((), ())), preferred_element_type=F32)


def _const_spec(shape):
    zeros = (0,) * len(shape)
    return pl.BlockSpec(shape, lambda *_: zeros, pipeline_mode=pl.Buffered(1))


def _params(n_grid_dims, sequential=False):
    sem = ("arbitrary" if sequential else "parallel",) * n_grid_dims
    return pltpu.CompilerParams(dimension_semantics=sem, vmem_limit_bytes=VMEM_LIMIT)


def _ffn_kernel(x_ref, g_ref, w1_ref, w3_ref, w2_ref, o_ref):
    x = x_ref[...]
    h = _rms(x, g_ref[...]).astype(BF16)
    a = _dot(h, w1_ref[...])
    b = _dot(h, w3_ref[...])
    act = (_silu(a) * b).astype(BF16)
    o_ref[...] = x + 0.5 * _dot(act, w2_ref[...])


def _ffn(x, g, w1, w3, w2, tm=512):
    t = x.shape[0]
    row = pl.BlockSpec((tm, D_MODEL), lambda i: (i, 0))
    return pl.pallas_call(
        _ffn_kernel,
        grid=(t // tm,),
        in_specs=[row, _const_spec((1, D_MODEL)), _const_spec((D_MODEL, D_FF)),
                  _const_spec((D_MODEL, D_FF)), _const_spec((D_FF, D_MODEL))],
        out_specs=row,
        out_shape=jax.ShapeDtypeStruct((t, D_MODEL), F32),
        compiler_params=_params(1),
        name="ffn",
    )(x, g, w1, w3, w2)


def _level_tables():
    c = CHUNK
    sums = np.zeros((len(FINE_HALVES) + 1, c, c), np.float32)
    masks = np.zeros((len(LEVEL_HALVES), c, c), np.float32)
    for li, b in enumerate(LEVEL_HALVES):
        for t in range(c):
            base = (t // (2 * b)) * 2 * b
            r = base + b - 1
            if t > r:
                masks[li, t, base:r + 1] = 1.0
            if b in FINE_HALVES:
                sums[li, t, (r + 1 if t > r else t + 1):(t + 1 if t > r else r + 1)] = 1.0
    for t in range(c):
        sums[len(FINE_HALVES), t, :t + 1] = 1.0
    return sums.reshape(-1, c), masks


PROJ_F32 = ("q", "k")
PROJ_BF16 = ("v", "lfh", "lfl", "og", "ga", "gb")
MERGE_NAMES = ("og", "ga", "gb", "u")


def _proj_pipe(x, g_ref, w_ref, lb_ref, dst):
    h = _rms(x, g_ref[...]).astype(BF16)
    lbp = lb_ref[...]
    e = jnp.exp(lbp - jnp.max(lbp, axis=0, keepdims=True))
    lb = e[0:1, :] / jnp.sum(e, axis=0, keepdims=True)
    yield

    def pieces(i):
        for lo in range(IN_SPLITS[i], IN_SPLITS[i + 1], PROJ_COLS):
            yield slice(lo - IN_SPLITS[i], lo - IN_SPLITS[i] + PROJ_COLS), _dot(h, w_ref[:, lo:lo + PROJ_COLS])

    for c, z in pieces(0):
        dst["q"][:, c] = _silu(z)
        yield
    for c, z in pieces(1):
        f = lb[:, c] + (1.0 - lb[:, c]) * jax.nn.sigmoid(z)
        lf = jnp.log(f) * LOG2E
        hi = lf.astype(BF16)
        dst["lfh"][:, c] = hi
        dst["lfl"][:, c] = (lf - hi.astype(F32)).astype(BF16)
        dst["k"][:, c] = 1.0 - f
        yield
    for c, z in pieces(2):
        dst["v"][:, c] = z.astype(BF16)
        yield
    for c, z in pieces(3):
        dst["og"][:, c] = _silu(z).astype(BF16)
        yield
    for c, z in pieces(4):
        dst["u"][:, c] = z.astype(BF16)
        yield
    for name, i in (("ga", 5), ("gb", 6)):
        for c, z in pieces(i):
            dst[name][:, c] = jax.nn.sigmoid(z).astype(BF16)
            yield


PROJ_PIECES = 1 + IN_SPLITS[-1] // PROJ_COLS


def _hgrn_pipe(src, o_dst, merge_dst, sums_ref, masks_ref, st_ref, new_sequence):
    for name in MERGE_NAMES:
        merge_dst[name][...] = src[name][...]
    yield

    row = lax.broadcasted_iota(jnp.int32, (CHUNK, 1), 0)
    odd = (row % 2) == 1
    head_cols = [slice(h * HEAD_DIM, (h + 1) * HEAD_DIM) for h in range(HEADS)]
    n_slabs = CHUNK // SLAB
    n_chunks = TILE // CHUNK

    def head_sums(x):
        return [jnp.sum(x[:, hc], axis=-1, keepdims=True) for hc in head_cols]

    def intra(c):
        rows = slice(c * CHUNK, (c + 1) * CHUNK)
        q = src["q"][rows, :]
        k = src["k"][rows, :]
        vb = src["v"][rows, :]
        v = vb.astype(F32)
        expo = _dot(sums_ref[...], jnp.concatenate([src["lfh"][rows, :], src["lfl"][rows, :]], axis=0))
        g = expo[len(FINE_HALVES) * CHUNK:, :]

        diag = head_sums(q * k)
        pair = head_sums(jnp.where(odd, q * (1.0 - k) * pltpu.roll(k, 1, 0), 0.0))
        v_prev = pltpu.roll(v, 1, 0)
        yield

        slabs = [[None] * n_slabs for _ in range(HEADS)]

        def add(h, j, val):
            slabs[h][j] = val if slabs[h][j] is None else slabs[h][j] + val

        for li, b in enumerate(LEVEL_HALVES):
            m = masks_ref[li]
            if b in FINE_HALVES:
                qk = jnp.where((row % (2 * b)) >= b, q, k)
                x = (qk * jnp.exp2(expo[li * CHUNK:(li + 1) * CHUNK, :])).astype(BF16)
            else:
                pieces = []
                for r0 in range(0, CHUNK, b):
                    r = (r0 // (2 * b)) * 2 * b + b - 1
                    if (r0 // b) % 2:
                        pieces.append(q[r0:r0 + b] * jnp.exp2(g[r0:r0 + b] - g[r:r + 1]))
                    else:
                        pieces.append(k[r0:r0 + b] * jnp.exp2(g[r:r + 1] - g[r0:r0 + b]))
                x = jnp.concatenate(pieces, axis=0).astype(BF16)
            if b % SLAB == 0:
                upper = [j for j in range(n_slabs) if (j * SLAB // b) % 2]
                lhs = jnp.concatenate([x[j * SLAB:(j + 1) * SLAB] for j in upper], axis=0)
                ml = jnp.concatenate([m[j * SLAB:(j + 1) * SLAB] for j in upper], axis=0)
            else:
                upper, lhs, ml = list(range(n_slabs)), x, m
            for h, hc in enumerate(head_cols):
                s = ml * _dot_nt(lhs[:, hc], x[:, hc])
                for i, j in enumerate(upper):
                    add(h, j, s[i * SLAB:(i + 1) * SLAB])
            yield
        q_state = (q * jnp.exp2(g)).astype(BF16)
        k_end = k * jnp.exp2(g[CHUNK - 1:CHUNK] - g)
        chunk_decay = jnp.exp2(g[CHUNK - 1:CHUNK])
        o_intra, k_end_t, decay_col = [], [], []
        for h, hc in enumerate(head_cols):
            scores = jnp.concatenate(slabs[h], axis=0)
            o_intra.append(_dot(scores.astype(BF16), vb[:, hc])
                           + diag[h] * v[:, hc] + pair[h] * v_prev[:, hc])
            k_end_t.append(k_end[:, hc].T.astype(BF16))
            decay_col.append(jnp.broadcast_to(chunk_decay[:, hc], (8, HEAD_DIM)).T[:, 0:1])
        yield
        return o_intra, q_state, k_end_t, decay_col, vb

    def inter(c, o_intra, q_state, k_end_t, decay_col, vb):
        rows = slice(c * CHUNK, (c + 1) * CHUNK)
        states = [st_ref[h] for h in range(HEADS)]
        if c == 0:
            states = [jnp.where(new_sequence, 0.0, s) for s in states]
        for h, hc in enumerate(head_cols):
            o_dst[rows, hc] = (o_intra[h] + _dot(q_state[:, hc], states[h].astype(BF16))).astype(BF16)
        for h, hc in enumerate(head_cols):
            st_ref[h] = decay_col[h] * states[h] + _dot(k_end_t[h], vb[:, hc])

    pending = yield from intra(0)
    for c in range(n_chunks):
        nxt = (yield from intra(c + 1)) if c + 1 < n_chunks else None
        inter(c, *pending)
        yield
        pending = nxt


HGRN_PIECES = 1 + (TILE // CHUNK) * (len(LEVEL_HALVES) + 3)


def _merge_pipe(x_ref, rows, o_src, src, onorm_ref, wa_ref, pw_ref, ps_ref, wb_ref, wo_ref, out_ref,
                ext_ref, halo_ref, new_sequence, pos0):
    half = D_MODEL // 2
    o = o_src[...].astype(F32)
    heads = []
    for h in range(HEADS):
        oh = o[:, h * HEAD_DIM:(h + 1) * HEAD_DIM]
        heads.append(oh * lax.rsqrt(jnp.mean(oh * oh, axis=-1, keepdims=True) + EPS))
        if h % 2:
            yield
    on = (jnp.concatenate(heads, axis=-1) * onorm_ref[...] * src["og"][...]).astype(BF16)
    ya = []
    for c in (slice(0, half), slice(half, D_MODEL)):
        ya.append(_dot(on, wa_ref[:, c]))
        yield

    u = src["u"][...].astype(F32)
    ext_ref[0:POOL_HALO, :] = jnp.where(new_sequence, 0.0, halo_ref[...])
    ext_ref[POOL_HALO:POOL_HALO + TILE, :] = u
    halo_ref[...] = u[TILE - POOL_HALO:TILE, :]
    pos = (pos0 + 1 + lax.broadcasted_iota(jnp.int32, (TILE, POOL_CH), 0)).astype(F32)
    groups = []
    for g, w in enumerate(POOL_WINDOWS):
        cols = slice(g * POOL_CH, (g + 1) * POOL_CH)
        tok = ext_ref[POOL_HALO:POOL_HALO + TILE, cols]
        win = tok
        for j in range(1, w):
            win = win + ext_ref[POOL_HALO - j:POOL_HALO - j + TILE, cols]
        pooled = win / jnp.minimum(pos, float(w)) - tok
        groups.append(_dot(pooled.astype(BF16), pw_ref[g]))
        yield
    mixed = (jnp.concatenate(groups, axis=-1) * ps_ref[...]).astype(BF16)
    y = []
    for i, c in enumerate((slice(0, half), slice(half, D_MODEL))):
        yb = _dot(mixed, wb_ref[:, c])
        y.append((src["ga"][:, c] * ya[i] + src["gb"][:, c] * yb).astype(BF16))
        yield
    y = jnp.concatenate(y, axis=-1)
    for c in (slice(0, half), slice(half, D_MODEL)):
        out_ref[rows, c] = x_ref[rows, c] + _dot(y, wo_ref[:, c])
        yield


MERGE_PIECES = HEADS // 2 + 2 + len(POOL_WINDOWS) + 2 + 2


def _interleave(pipes):
    done = [0] * len(pipes)
    alive = [True] * len(pipes)
    while any(alive):
        i = min((i for i in range(len(pipes)) if alive[i]), key=lambda i: (done[i] + 1) / pipes[i][1])
        try:
            next(pipes[i][0])
            done[i] += 1
        except StopIteration:
            alive[i] = False


def _mixer_kernel(xp_ref, xm_ref, g_ref, w_ref, lb_ref, sums_ref, masks_ref, onorm_ref, wa_ref, pw_ref, ps_ref,
                  wb_ref, wo_ref, out_ref, *scratch, tiles_per_seq):
    scratch = list(scratch)

    def take(names):
        return {n: scratch.pop(0) for n in names}

    proj_buf = [take(PROJ_F32 + PROJ_BF16 + ("u",)) for _ in range(2)]
    merge_buf = [take(MERGE_NAMES) for _ in range(2)]
    o_buf = [scratch.pop(0) for _ in range(2)]
    st_ref, ext_ref, halo_ref = scratch
    i = pl.program_id(0)

    @pl.when(i == 0)
    def _():
        for ref in list(proj_buf[1].values()) + list(merge_buf[0].values()) + [o_buf[0], st_ref, halo_ref]:
            ref[...] = jnp.zeros_like(ref)

    for a in range(2):
        g = 2 * i + a
        rows = slice(a * TILE, (a + 1) * TILE)
        seq_h = lax.rem(g - 1 + tiles_per_seq, tiles_per_seq)
        seq_m = lax.rem(g - 2 + tiles_per_seq, tiles_per_seq)
        _interleave([
            (_proj_pipe(xp_ref[rows, :], g_ref, w_ref, lb_ref, proj_buf[a]), PROJ_PIECES),
            (_hgrn_pipe(proj_buf[1 - a], o_buf[1 - a], merge_buf[1 - a], sums_ref, masks_ref, st_ref,
                        seq_h == 0), HGRN_PIECES),
            (_merge_pipe(xm_ref, rows, o_buf[a], merge_buf[a], onorm_ref, wa_ref, pw_ref, ps_ref, wb_ref, wo_ref,
                         out_ref, ext_ref, halo_ref, seq_m == 0, seq_m * TILE), MERGE_PIECES),
        ])


def _mixer(x, g, w_in, hgrn_lb, onorm, wa, pw, ps, wb, wo, seq):
    t = x.shape[0]
    n_steps = t // PAIR + 1
    sums, masks = _level_tables()
    sums2 = jnp.asarray(np.concatenate([sums, sums], axis=1), BF16)
    tile = lambda width, dtype: pltpu.VMEM((TILE, width), dtype)
    proj_slot = ([tile(HG_WIDTH, F32)] * len(PROJ_F32) + [tile(HG_WIDTH, BF16)] * len(PROJ_BF16)
                 + [tile(POOL_WIDTH, BF16)])
    merge_slot = [tile(HG_WIDTH, BF16)] * 3 + [tile(POOL_WIDTH, BF16)]
    return pl.pallas_call(
        functools.partial(_mixer_kernel, tiles_per_seq=seq // TILE),
        grid=(n_steps,),
        in_specs=[pl.BlockSpec((PAIR, D_MODEL), lambda i: (jnp.minimum(i, n_steps - 2), 0)),
                  pl.BlockSpec((PAIR, D_MODEL), lambda i: (jnp.maximum(i - 1, 0), 0)),
                  _const_spec((1, D_MODEL)), _const_spec((D_MODEL, IN_SPLITS[-1])), _const_spec(hgrn_lb.shape),
                  _const_spec(sums2.shape), _const_spec(masks.shape),
                  _const_spec((1, HG_WIDTH)), _const_spec((HG_WIDTH, D_MODEL)),
                  _const_spec((len(POOL_WINDOWS), POOL_CH, POOL_CH)), _const_spec((1, POOL_WIDTH)),
                  _const_spec((POOL_WIDTH, D_MODEL)), _const_spec((D_MODEL, D_MODEL))],
        out_specs=pl.BlockSpec((PAIR, D_MODEL), lambda i: (jnp.maximum(i - 1, 0), 0)),
        out_shape=jax.ShapeDtypeStruct((t, D_MODEL), F32),
        scratch_shapes=(proj_slot * 2 + merge_slot * 2 + [tile(HG_WIDTH, BF16)] * 2
                        + [pltpu.VMEM((HEADS, HEAD_DIM, HEAD_DIM), F32),
                           pltpu.VMEM((POOL_HALO + TILE, POOL_WIDTH), F32),
                           pltpu.VMEM((POOL_HALO, POOL_WIDTH), F32)]),
        compiler_params=_params(1, sequential=True),
        name="mixer",
    )(x, x, g, w_in, hgrn_lb, sums2, jnp.asarray(masks), onorm, wa, pw, ps, wb, wo)


def _ffn_ple_kernel(x_ref, p_ref, g_ref, w1_ref, w3_ref, w2_ref, gn_ref, wg_ref, wp_ref, pn_ref, fn_ref, o_ref):
    x = x_ref[...]
    h = _rms(x, g_ref[...]).astype(BF16)
    act = (_silu(_dot(h, w1_ref[...])) * _dot(h, w3_ref[...])).astype(BF16)
    x = x + 0.5 * _dot(act, w2_ref[...])
    gate = jax.nn.sigmoid(_dot(_rms(x, gn_ref[...]).astype(BF16), wg_ref[...]))
    e = _rms(_dot(p_ref[...].astype(BF16), wp_ref[...]), pn_ref[...])
    o_ref[...] = _rms(x + gate * e, fn_ref[...])


def _ffn_ple(x, p, g, w1, w3, w2, gn, wg, wp, pn, fn, tm=512):
    t = x.shape[0]
    row = pl.BlockSpec((tm, D_MODEL), lambda i: (i, 0))
    vec = _const_spec((1, D_MODEL))
    return pl.pallas_call(
        _ffn_ple_kernel,
        grid=(t // tm,),
        in_specs=[row, pl.BlockSpec((tm, PLE_DIM), lambda i: (i, 0)), vec, _const_spec((D_MODEL, D_FF)),
                  _const_spec((D_MODEL, D_FF)), _const_spec((D_FF, D_MODEL)), vec,
                  _const_spec((D_MODEL, D_MODEL)), _const_spec((PLE_DIM, D_MODEL)), vec, vec],
        out_specs=row,
        out_shape=jax.ShapeDtypeStruct((t, D_MODEL), F32),
        compiler_params=_params(1),
        name="ffn_ple",
    )(x, p, g, w1, w3, w2, gn, wg, wp, pn, fn)


def kernel(x, p, ffn1_norm, ffn1_w1, ffn1_w3, ffn1_w2, mix_norm, w_in, hgrn_lb, hgrn_onorm, w_branch_a, pool_w, pool_scale, w_branch_b, w_out, ffn2_norm, ffn2_w1, ffn2_w3, ffn2_w2, ple_norm, ple_w_gate, ple_w_proj, ple_post_norm, final_norm):
    batch, seq, d = x.shape
    assert d == D_MODEL and p.shape[0] == 1 and hgrn_lb.shape == (2, HG_WIDTH)
    assert seq % PAIR == 0
    t = batch * seq
    bf = lambda w: w.astype(BF16)
    vec = lambda g: g.reshape(1, -1)

    xt = x.reshape(t, d)
    xt = _ffn(xt, vec(ffn1_norm[0]), bf(ffn1_w1[0]), bf(ffn1_w3[0]), bf(ffn1_w2[0]))
    xt = _mixer(xt, vec(mix_norm[0]), bf(w_in[0]), hgrn_lb, vec(hgrn_onorm[0]), bf(w_branch_a[0]), bf(pool_w[0]),
                vec(pool_scale[0]), bf(w_branch_b[0]), bf(w_out[0]), seq)
    out = _ffn_ple(xt, p[0].reshape(t, PLE_DIM), vec(ffn2_norm[0]), bf(ffn2_w1[0]), bf(ffn2_w3[0]), bf(ffn2_w2[0]),
                   vec(ple_norm[0]), bf(ple_w_gate[0]), bf(ple_w_proj[0]), vec(ple_post_norm[0]), vec(final_norm))
    return out.reshape(batch, seq, d)
```

```python
import functools

import numpy as np
import jax
import jax.numpy as jnp
from jax import lax
from jax.experimental import pallas as pl
from jax.experimental.pallas import tpu as pltpu

D_MODEL = 1024
D_FF = 2816
PLE_DIM = 256
HEADS = 8
HEAD_DIM = 128
HG_WIDTH = HEADS * HEAD_DIM
POOL_WINDOWS = (2, 4, 8, 16)
POOL_CH = 128
POOL_WIDTH = len(POOL_WINDOWS) * POOL_CH
POOL_HALO = 16
IN_SPLITS = (0, 1024, 2048, 3072, 4096, 4608, 5632, 6656)
EPS = 1e-6
LOG2E = 1.4426950408889634

CHUNK = 64
FINE_HALVES = (2, 4)
LEVEL_HALVES = FINE_HALVES + (8, 16, 32)
SLAB = 16

TILE = 256
PAIR = 2 * TILE
PROJ_COLS = 256
MERGE_COLS = 256

V7X_VMEM_BYTES = 64 * 1024 * 1024
VMEM_LIMIT = V7X_VMEM_BYTES - 8 * 1024 * 1024

BF16 = jnp.bfloat16
F32 = jnp.float32


def _rms(x, g):
    return x * lax.rsqrt(jnp.mean(x * x, axis=-1, keepdims=True) + EPS) * g


def _silu(x):
    return x * jax.nn.sigmoid(x)


def _dot(a, b):
    return jnp.dot(a, b, preferred_element_type=F32)


def _dot_nt(a, b):
    return lax.dot_general(a, b, (((1,), (1,)), ((), ())), preferred_element_type=F32)


def _const_spec(shape):
    zeros = (0,) * len(shape)
    return pl.BlockSpec(shape, lambda *_: zeros, pipeline_mode=pl.Buffered(1))


def _params(n_grid_dims, sequential=False):
    sem = ("arbitrary" if sequential else "parallel",) * n_grid_dims
    return pltpu.CompilerParams(dimension_semantics=sem, vmem_limit_bytes=VMEM_LIMIT)


def _ffn_kernel(x_ref, g_ref, w1_ref, w3_ref, w2_ref, o_ref):
    x = x_ref[...]
    h = _rms(x, g_ref[...]).astype(BF16)
    a = _dot(h, w1_ref[...])
    b = _dot(h, w3_ref[...])
    act = (_silu(a) * b).astype(BF16)
    o_ref[...] = x + 0.5 * _dot(act, w2_ref[...])


def _ffn(x, g, w1, w3, w2, tm=512):
    t = x.shape[0]
    row = pl.BlockSpec((tm, D_MODEL), lambda i: (i, 0))
    return pl.pallas_call(
        _ffn_kernel,
        grid=(t // tm,),
        in_specs=[row, _const_spec((1, D_MODEL)), _const_spec((D_MODEL, D_FF)),
                  _const_spec((D_MODEL, D_FF)), _const_spec((D_FF, D_MODEL))],
        out_specs=row,
        out_shape=jax.ShapeDtypeStruct((t, D_MODEL), F32),
        compiler_params=_params(1),
        name="ffn",
    )(x, g, w1, w3, w2)


def _level_tables():
    c = CHUNK
    sums = np.zeros((len(FINE_HALVES) + 1, c, c), np.float32)
    masks = np.zeros((len(LEVEL_HALVES), c, c), np.float32)
    for li, b in enumerate(LEVEL_HALVES):
        for t in range(c):
            base = (t // (2 * b)) * 2 * b
            r = base + b - 1
            if t > r:
                masks[li, t, base:r + 1] = 1.0
            if b in FINE_HALVES:
                sums[li, t, (r + 1 if t > r else t + 1):(t + 1 if t > r else r + 1)] = 1.0
    for t in range(c):
        sums[len(FINE_HALVES), t, :t + 1] = 1.0
    return sums.reshape(-1, c), masks


PROJ_F32 = ("q", "k")
PROJ_BF16 = ("v", "lfh", "lfl", "og", "ga", "gb")
MERGE_NAMES = ("og", "ga", "gb", "u")


def _proj_pipe(x, g_ref, w_ref, lb_ref, dst):
    h = _rms(x, g_ref[...]).astype(BF16)
    lbp = lb_ref[...]
    e = jnp.exp(lbp - jnp.max(lbp, axis=0, keepdims=True))
    lb = e[0:1, :] / jnp.sum(e, axis=0, keepdims=True)
    yield

    def pieces(i):
        for lo in range(IN_SPLITS[i], IN_SPLITS[i + 1], PROJ_COLS):
            yield slice(lo - IN_SPLITS[i], lo - IN_SPLITS[i] + PROJ_COLS), _dot(h, w_ref[:, lo:lo + PROJ_COLS])

    for c, z in pieces(0):
        dst["q"][:, c] = _silu(z)
        yield
    for c, z in pieces(1):
        f = lb[:, c] + (1.0 - lb[:, c]) * jax.nn.sigmoid(z)
        lf = jnp.log(f) * LOG2E
        hi = lf.astype(BF16)
        dst["lfh"][:, c] = hi
        dst["lfl"][:, c] = (lf - hi.astype(F32)).astype(BF16)
        dst["k"][:, c] = 1.0 - f
        yield
    for c, z in pieces(2):
        dst["v"][:, c] = z.astype(BF16)
        yield
    for c, z in pieces(3):
        dst["og"][:, c] = _silu(z).astype(BF16)
        yield
    for c, z in pieces(4):
        dst["u"][:, c] = z.astype(BF16)
        yield
    for name, i in (("ga", 5), ("gb", 6)):
        for c, z in pieces(i):
            dst[name][:, c] = jax.nn.sigmoid(z).astype(BF16)
            yield


PROJ_PIECES = 1 + IN_SPLITS[-1] // PROJ_COLS


def _hgrn_pipe(src, o_dst, merge_dst, sums_ref, masks_ref, st_ref, new_sequence):
    for name in MERGE_NAMES:
        merge_dst[name][...] = src[name][...]
    yield

    row = lax.broadcasted_iota(jnp.int32, (CHUNK, 1), 0)
    odd = (row % 2) == 1
    head_cols = [slice(h * HEAD_DIM, (h + 1) * HEAD_DIM) for h in range(HEADS)]
    n_slabs = CHUNK // SLAB
    n_chunks = TILE // CHUNK

    def head_sums(x):
        return [jnp.sum(x[:, hc], axis=-1, keepdims=True) for hc in head_cols]

    def prev_row(x):
        return jnp.concatenate([pltpu.roll(x[r:r + 8], 1, 0) for r in range(0, CHUNK, 8)], axis=0)

    def intra(c):
        rows = slice(c * CHUNK, (c + 1) * CHUNK)
        q = src["q"][rows, :]
        k = src["k"][rows, :]
        vb = src["v"][rows, :]
        v = vb.astype(F32)
        expo = _dot(sums_ref[...], jnp.concatenate([src["lfh"][rows, :], src["lfl"][rows, :]], axis=0))
        g = expo[len(FINE_HALVES) * CHUNK:, :]

        diag = head_sums(q * k)
        pair = head_sums(jnp.where(odd, q * (1.0 - k) * prev_row(k), 0.0))
        v_prev = prev_row(v)
        yield

        slabs = [[None] * n_slabs for _ in range(HEADS)]

        def add(h, j, val):
            slabs[h][j] = val if slabs[h][j] is None else slabs[h][j] + val

        for li, b in enumerate(LEVEL_HALVES):
            m = masks_ref[li]
            if b in FINE_HALVES:
                qk = jnp.where((row % (2 * b)) >= b, q, k)
                x = (qk * jnp.exp2(expo[li * CHUNK:(li + 1) * CHUNK, :])).astype(BF16)
            else:
                pieces = []
                for r0 in range(0, CHUNK, b):
                    r = (r0 // (2 * b)) * 2 * b + b - 1
                    if (r0 // b) % 2:
                        pieces.append(q[r0:r0 + b] * jnp.exp2(g[r0:r0 + b] - g[r:r + 1]))
                    else:
                        pieces.append(k[r0:r0 + b] * jnp.exp2(g[r:r + 1] - g[r0:r0 + b]))
                x = jnp.concatenate(pieces, axis=0).astype(BF16)
            if b % SLAB == 0:
                upper = [j for j in range(n_slabs) if (j * SLAB // b) % 2]
                lhs = jnp.concatenate([x[j * SLAB:(j + 1) * SLAB] for j in upper], axis=0)
                ml = jnp.concatenate([m[j * SLAB:(j + 1) * SLAB] for j in upper], axis=0)
            else:
                upper, lhs, ml = list(range(n_slabs)), x, m
            for h, hc in enumerate(head_cols):
                s = ml * _dot_nt(lhs[:, hc], x[:, hc])
                for i, j in enumerate(upper):
                    add(h, j, s[i * SLAB:(i + 1) * SLAB])
            yield
        q_state = (q * jnp.exp2(g)).astype(BF16)
        k_end = k * jnp.exp2(g[CHUNK - 1:CHUNK] - g)
        chunk_decay = jnp.exp2(g[CHUNK - 1:CHUNK])
        o_intra, k_end_t, decay_col = [], [], []
        for h, hc in enumerate(head_cols):
            scores = jnp.concatenate(slabs[h], axis=0)
            o_intra.append(_dot(scores.astype(BF16), vb[:, hc])
                           + diag[h] * v[:, hc] + pair[h] * v_prev[:, hc])
            k_end_t.append(k_end[:, hc].T.astype(BF16))
            decay_col.append(jnp.broadcast_to(chunk_decay[:, hc], (8, HEAD_DIM)).T[:, 0:1])
        yield
        return o_intra, q_state, k_end_t, decay_col, vb

    def inter(c, o_intra, q_state, k_end_t, decay_col, vb):
        rows = slice(c * CHUNK, (c + 1) * CHUNK)
        states = [st_ref[h] for h in range(HEADS)]
        if c == 0:
            states = [jnp.where(new_sequence, 0.0, s) for s in states]
        for h, hc in enumerate(head_cols):
            o_dst[rows, hc] = (o_intra[h] + _dot(q_state[:, hc], states[h].astype(BF16))).astype(BF16)
        for h, hc in enumerate(head_cols):
            st_ref[h] = decay_col[h] * states[h] + _dot(k_end_t[h], vb[:, hc])

    pending = yield from intra(0)
    for c in range(n_chunks):
        nxt = (yield from intra(c + 1)) if c + 1 < n_chunks else None
        inter(c, *pending)
        yield
        pending = nxt


HGRN_PIECES = 1 + (TILE // CHUNK) * (len(LEVEL_HALVES) + 3)


def _merge_pipe(x_ref, rows, o_src, src, onorm_ref, wa_ref, pw_ref, ps_ref, wb_ref, wo_ref, out_ref,
                ext_ref, halo_ref, new_sequence, pos0):
    col_pieces = [slice(lo, lo + MERGE_COLS) for lo in range(0, D_MODEL, MERGE_COLS)]
    o = o_src[...].astype(F32)
    heads = []
    for h in range(HEADS):
        oh = o[:, h * HEAD_DIM:(h + 1) * HEAD_DIM]
        heads.append(oh * lax.rsqrt(jnp.mean(oh * oh, axis=-1, keepdims=True) + EPS))
        if h % 2:
            yield
    on = (jnp.concatenate(heads, axis=-1) * onorm_ref[...] * src["og"][...]).astype(BF16)
    ya = []
    for c in col_pieces:
        ya.append(_dot(on, wa_ref[:, c]))
        yield

    u = src["u"][...].astype(F32)
    ext_ref[0:POOL_HALO, :] = jnp.where(new_sequence, 0.0, halo_ref[...])
    ext_ref[POOL_HALO:POOL_HALO + TILE, :] = u
    halo_ref[...] = u[TILE - POOL_HALO:TILE, :]
    pos = (pos0 + 1 + lax.broadcasted_iota(jnp.int32, (TILE, POOL_CH), 0)).astype(F32)
    groups = []
    for g, w in enumerate(POOL_WINDOWS):
        cols = slice(g * POOL_CH, (g + 1) * POOL_CH)
        tok = ext_ref[POOL_HALO:POOL_HALO + TILE, cols]
        win = tok
        for j in range(1, w):
            win = win + ext_ref[POOL_HALO - j:POOL_HALO - j + TILE, cols]
        pooled = win / jnp.minimum(pos, float(w)) - tok
        groups.append(_dot(pooled.astype(BF16), pw_ref[g]))
        yield
    mixed = (jnp.concatenate(groups, axis=-1) * ps_ref[...]).astype(BF16)
    y = []
    for i, c in enumerate(col_pieces):
        yb = _dot(mixed, wb_ref[:, c])
        y.append((src["ga"][:, c] * ya[i] + src["gb"][:, c] * yb).astype(BF16))
        yield
    y = jnp.concatenate(y, axis=-1)
    for c in col_pieces:
        out_ref[rows, c] = x_ref[rows, c] + _dot(y, wo_ref[:, c])
        yield


MERGE_PIECES = HEADS // 2 + len(POOL_WINDOWS) + 3 * (D_MODEL // MERGE_COLS)


def _interleave(pipes):
    done = [0] * len(pipes)
    alive = [True] * len(pipes)
    while any(alive):
        i = min((i for i in range(len(pipes)) if alive[i]), key=lambda i: (done[i] + 1) / pipes[i][1])
        try:
            next(pipes[i][0])
            done[i] += 1
        except StopIteration:
            alive[i] = False


def _mixer_kernel(xp_ref, xm_ref, g_ref, w_ref, lb_ref, sums_ref, masks_ref, onorm_ref, wa_ref, pw_ref, ps_ref,
                  wb_ref, wo_ref, out_ref, *scratch, tiles_per_seq):
    scratch = list(scratch)

    def take(names):
        return {n: scratch.pop(0) for n in names}

    proj_buf = [take(PROJ_F32 + PROJ_BF16 + ("u",)) for _ in range(2)]
    merge_buf = [take(MERGE_NAMES) for _ in range(2)]
    o_buf = [scratch.pop(0) for _ in range(2)]
    st_ref, ext_ref, halo_ref = scratch
    i = pl.program_id(0)

    @pl.when(i == 0)
    def _():
        for ref in list(proj_buf[1].values()) + list(merge_buf[0].values()) + [o_buf[0], st_ref, halo_ref]:
            ref[...] = jnp.zeros_like(ref)

    for a in range(2):
        g = 2 * i + a
        rows = slice(a * TILE, (a + 1) * TILE)
        seq_h = lax.rem(g - 1 + tiles_per_seq, tiles_per_seq)
        seq_m = lax.rem(g - 2 + tiles_per_seq, tiles_per_seq)
        _interleave([
            (_proj_pipe(xp_ref[rows, :], g_ref, w_ref, lb_ref, proj_buf[a]), PROJ_PIECES),
            (_hgrn_pipe(proj_buf[1 - a], o_buf[1 - a], merge_buf[1 - a], sums_ref, masks_ref, st_ref,
                        seq_h == 0), HGRN_PIECES),
            (_merge_pipe(xm_ref, rows, o_buf[a], merge_buf[a], onorm_ref, wa_ref, pw_ref, ps_ref, wb_ref, wo_ref,
                         out_ref, ext_ref, halo_ref, seq_m == 0, seq_m * TILE), MERGE_PIECES),
        ])


def _mixer(x, g, w_in, hgrn_lb, onorm, wa, pw, ps, wb, wo, seq):
    t = x.shape[0]
    n_steps = t // PAIR + 1
    sums, masks = _level_tables()
    sums2 = jnp.asarray(np.concatenate([sums, sums], axis=1), BF16)
    tile = lambda width, dtype: pltpu.VMEM((TILE, width), dtype)
    proj_slot = ([tile(HG_WIDTH, F32)] * len(PROJ_F32) + [tile(HG_WIDTH, BF16)] * len(PROJ_BF16)
                 + [tile(POOL_WIDTH, BF16)])
    merge_slot = [tile(HG_WIDTH, BF16)] * 3 + [tile(POOL_WIDTH, BF16)]
    return pl.pallas_call(
        functools.partial(_mixer_kernel, tiles_per_seq=seq // TILE),
        grid=(n_steps,),
        in_specs=[pl.BlockSpec((PAIR, D_MODEL), lambda i: (jnp.minimum(i, n_steps - 2), 0)),
                  pl.BlockSpec((PAIR, D_MODEL), lambda i: (jnp.maximum(i - 1, 0), 0)),
                  _const_spec((1, D_MODEL)), _const_spec((D_MODEL, IN_SPLITS[-1])), _const_spec(hgrn_lb.shape),
                  _const_spec(sums2.shape), _const_spec(masks.shape),
                  _const_spec((1, HG_WIDTH)), _const_spec((HG_WIDTH, D_MODEL)),
                  _const_spec((len(POOL_WINDOWS), POOL_CH, POOL_CH)), _const_spec((1, POOL_WIDTH)),
                  _const_spec((POOL_WIDTH, D_MODEL)), _const_spec((D_MODEL, D_MODEL))],
        out_specs=pl.BlockSpec((PAIR, D_MODEL), lambda i: (jnp.maximum(i - 1, 0), 0)),
        out_shape=jax.ShapeDtypeStruct((t, D_MODEL), F32),
        scratch_shapes=(proj_slot * 2 + merge_slot * 2 + [tile(HG_WIDTH, BF16)] * 2
                        + [pltpu.VMEM((HEADS, HEAD_DIM, HEAD_DIM), F32),
                           pltpu.VMEM((POOL_HALO + TILE, POOL_WIDTH), F32),
                           pltpu.VMEM((POOL_HALO, POOL_WIDTH), F32)]),
        compiler_params=_params(1, sequential=True),
        name="mixer",
    )(x, x, g, w_in, hgrn_lb, sums2, jnp.asarray(masks), onorm, wa, pw, ps, wb, wo)


def _ffn_ple_kernel(x_ref, p_ref, g_ref, w1_ref, w3_ref, w2_ref, gn_ref, wg_ref, wp_ref, pn_ref, fn_ref, o_ref):
    x = x_ref[...]
    h = _rms(x, g_ref[...]).astype(BF16)
    act = (_silu(_dot(h, w1_ref[...])) * _dot(h, w3_ref[...])).astype(BF16)
    x = x + 0.5 * _dot(act, w2_ref[...])
    gate = jax.nn.sigmoid(_dot(_rms(x, gn_ref[...]).astype(BF16), wg_ref[...]))
    e = _rms(_dot(p_ref[...].astype(BF16), wp_ref[...]), pn_ref[...])
    o_ref[...] = _rms(x + gate * e, fn_ref[...])


def _ffn_ple(x, p, g, w1, w3, w2, gn, wg, wp, pn, fn, tm=512):
    t = x.shape[0]
    row = pl.BlockSpec((tm, D_MODEL), lambda i: (i, 0))
    vec = _const_spec((1, D_MODEL))
    return pl.pallas_call(
        _ffn_ple_kernel,
        grid=(t // tm,),
        in_specs=[row, pl.BlockSpec((tm, PLE_DIM), lambda i: (i, 0)), vec, _const_spec((D_MODEL, D_FF)),
                  _const_spec((D_MODEL, D_FF)), _const_spec((D_FF, D_MODEL)), vec,
                  _const_spec((D_MODEL, D_MODEL)), _const_spec((PLE_DIM, D_MODEL)), vec, vec],
        out_specs=row,
        out_shape=jax.ShapeDtypeStruct((t, D_MODEL), F32),
        compiler_params=_params(1),
        name="ffn_ple",
    )(x, p, g, w1, w3, w2, gn, wg, wp, pn, fn)


def kernel(x, p, ffn1_norm, ffn1_w1, ffn1_w3, ffn1_w2, mix_norm, w_in, hgrn_lb, hgrn_onorm, w_branch_a, pool_w, pool_scale, w_branch_b, w_out, ffn2_norm, ffn2_w1, ffn2_w3, ffn2_w2, ple_norm, ple_w_gate, ple_w_proj, ple_post_norm, final_norm):
    batch, seq, d = x.shape
    assert d == D_MODEL and p.shape[0] == 1 and hgrn_lb.shape == (2, HG_WIDTH)
    assert seq % PAIR == 0
    t = batch * seq
    bf = lambda w: w.astype(BF16)
    vec = lambda g: g.reshape(1, -1)

    xt = x.reshape(t, d)
    xt = _ffn(xt, vec(ffn1_norm[0]), bf(ffn1_w1[0]), bf(ffn1_w3[0]), bf(ffn1_w2[0]))
    xt = _mixer(xt, vec(mix_norm[0]), bf(w_in[0]), hgrn_lb, vec(hgrn_onorm[0]), bf(w_branch_a[0]), bf(pool_w[0]),
                vec(pool_scale[0]), bf(w_branch_b[0]), bf(w_out[0]), seq)
    out = _ffn_ple(xt, p[0].reshape(t, PLE_DIM), vec(ffn2_norm[0]), bf(ffn2_w1[0]), bf(ffn2_w3[0]), bf(ffn2_w2[0]),
                   vec(ple_norm[0]), bf(ple_w_gate[0]), bf(ple_w_proj[0]), vec(ple_post_norm[0]), vec(final_norm))
    return out.reshape(batch, seq, d)
```

```python
import functools

import numpy as np
import jax
import jax.numpy as jnp
from jax import lax
from jax.experimental import pallas as pl
from jax.experimental.pallas import tpu as pltpu

D_MODEL = 1024
D_FF = 2816
PLE_DIM = 256
HEADS = 8
HEAD_DIM = 128
HG_WIDTH = HEADS * HEAD_DIM
POOL_WINDOWS = (2, 4, 8, 16)
POOL_CH = 128
POOL_WIDTH = len(POOL_WINDOWS) * POOL_CH
POOL_HALO = 16
IN_SPLITS = (0, 1024, 2048, 3072, 4096, 4608, 5632, 6656)
EPS = 1e-6
LOG2E = 1.4426950408889634

CHUNK = 64
FINE_HALVES = (2, 4)
LEVEL_HALVES = FINE_HALVES + (8, 16, 32)
SLAB = 16

TILE = 256
PAIR = 2 * TILE
PROJ_COLS = 256
MERGE_COLS = 256

V7X_VMEM_BYTES = 64 * 1024 * 1024
VMEM_LIMIT = V7X_VMEM_BYTES - 8 * 1024 * 1024

BF16 = jnp.bfloat16
F32 = jnp.float32


def _rms(x, g):
    return x * lax.rsqrt(jnp.mean(x * x, axis=-1, keepdims=True) + EPS) * g


def _silu(x):
    return x * jax.nn.sigmoid(x)


def _dot(a, b):
    return jnp.dot(a, b, preferred_element_type=F32)


def _dot_nt(a, b):
    return lax.dot_general(a, b, (((1,), (1,)), ((), ())), preferred_element_type=F32)


def _const_spec(shape):
    zeros = (0,) * len(shape)
    return pl.BlockSpec(shape, lambda *_: zeros, pipeline_mode=pl.Buffered(1))


def _params(n_grid_dims, sequential=False):
    sem = ("arbitrary" if sequential else "parallel",) * n_grid_dims
    return pltpu.CompilerParams(dimension_semantics=sem, vmem_limit_bytes=VMEM_LIMIT)


def _cast_specs(weights, n_steps):
    in_specs, out_specs, out_shapes = [], [], []
    for w in weights:
        rows, cols = w.shape
        slab = next(s for s in range(SLAB, rows + 1, SLAB) if rows % s == 0 and s * n_steps >= rows)
        last = rows // slab - 1
        index = lambda i, last=last: (jnp.minimum(i, last), 0)
        in_specs.append(pl.BlockSpec((slab, cols), index))
        out_specs.append(pl.BlockSpec((slab, cols), index))
        out_shapes.append(jax.ShapeDtypeStruct((rows, cols), BF16))
    return in_specs, out_specs, out_shapes


def _cast_slabs(src_refs, dst_refs):
    for src, dst in zip(src_refs, dst_refs):
        dst[...] = src[...].astype(BF16)


def _ffn_kernel(x_ref, g_ref, w1_ref, w3_ref, w2_ref, *rest):
    n_cast = len(rest) // 2
    o_ref = rest[n_cast]
    _cast_slabs(rest[:n_cast], rest[n_cast + 1:])
    x = x_ref[...]
    h = _rms(x, g_ref[...]).astype(BF16)
    a = _dot(h, w1_ref[...])
    b = _dot(h, w3_ref[...])
    act = (_silu(a) * b).astype(BF16)
    o_ref[...] = x + 0.5 * _dot(act, w2_ref[...])


def _ffn(x, g, w1, w3, w2, cast=(), tm=512):
    t = x.shape[0]
    row = pl.BlockSpec((tm, D_MODEL), lambda i: (i, 0))
    cast_in, cast_out, cast_shapes = _cast_specs(cast, t // tm)
    return pl.pallas_call(
        _ffn_kernel,
        grid=(t // tm,),
        in_specs=[row, _const_spec((1, D_MODEL)), _const_spec((D_MODEL, D_FF)),
                  _const_spec((D_MODEL, D_FF)), _const_spec((D_FF, D_MODEL))] + cast_in,
        out_specs=[row] + cast_out,
        out_shape=[jax.ShapeDtypeStruct((t, D_MODEL), F32)] + cast_shapes,
        compiler_params=_params(1),
        name="ffn",
    )(x, g, w1, w3, w2, *cast)


def _level_tables():
    c = CHUNK
    sums = np.zeros((len(FINE_HALVES) + 1, c, c), np.float32)
    masks = np.zeros((len(LEVEL_HALVES), c, c), np.float32)
    for li, b in enumerate(LEVEL_HALVES):
        for t in range(c):
            base = (t // (2 * b)) * 2 * b
            r = base + b - 1
            if t > r:
                masks[li, t, base:r + 1] = 1.0
            if b in FINE_HALVES:
                sums[li, t, (r + 1 if t > r else t + 1):(t + 1 if t > r else r + 1)] = 1.0
    for t in range(c):
        sums[len(FINE_HALVES), t, :t + 1] = 1.0
    return sums.reshape(-1, c), masks


PROJ_F32 = ("q", "k")
PROJ_BF16 = ("v", "lfh", "lfl", "og", "ga", "gb")
MERGE_NAMES = ("og", "ga", "gb", "u")


def _proj_pipe(x, g_ref, w_ref, lb_ref, dst):
    h = _rms(x, g_ref[...]).astype(BF16)
    lbp = lb_ref[...]
    e = jnp.exp(lbp - jnp.max(lbp, axis=0, keepdims=True))
    lb = e[0:1, :] / jnp.sum(e, axis=0, keepdims=True)
    yield

    def pieces(i):
        for lo in range(IN_SPLITS[i], IN_SPLITS[i + 1], PROJ_COLS):
            yield slice(lo - IN_SPLITS[i], lo - IN_SPLITS[i] + PROJ_COLS), _dot(h, w_ref[:, lo:lo + PROJ_COLS])

    for c, z in pieces(0):
        dst["q"][:, c] = _silu(z)
        yield
    for c, z in pieces(1):
        f = lb[:, c] + (1.0 - lb[:, c]) * jax.nn.sigmoid(z)
        lf = jnp.log(f) * LOG2E
        hi = lf.astype(BF16)
        dst["lfh"][:, c] = hi
        dst["lfl"][:, c] = (lf - hi.astype(F32)).astype(BF16)
        dst["k"][:, c] = 1.0 - f
        yield
    for c, z in pieces(2):
        dst["v"][:, c] = z.astype(BF16)
        yield
    for c, z in pieces(3):
        dst["og"][:, c] = _silu(z).astype(BF16)
        yield
    for c, z in pieces(4):
        dst["u"][:, c] = z.astype(BF16)
        yield
    for name, i in (("ga", 5), ("gb", 6)):
        for c, z in pieces(i):
            dst[name][:, c] = jax.nn.sigmoid(z).astype(BF16)
            yield


PROJ_PIECES = 1 + IN_SPLITS[-1] // PROJ_COLS


def _hgrn_pipe(src, o_dst, merge_dst, sums_ref, masks_ref, st_ref, new_sequence):
    for name in MERGE_NAMES:
        merge_dst[name][...] = src[name][...]
    yield

    row = lax.broadcasted_iota(jnp.int32, (CHUNK, 1), 0)
    odd = (row % 2) == 1
    head_cols = [slice(h * HEAD_DIM, (h + 1) * HEAD_DIM) for h in range(HEADS)]
    n_slabs = CHUNK // SLAB
    n_chunks = TILE // CHUNK

    def head_sums(x):
        return [jnp.sum(x[:, hc], axis=-1, keepdims=True) for hc in head_cols]

    def prev_row(x):
        return jnp.concatenate([pltpu.roll(x[r:r + 8], 1, 0) for r in range(0, CHUNK, 8)], axis=0)

    def intra(c):
        rows = slice(c * CHUNK, (c + 1) * CHUNK)
        q = src["q"][rows, :]
        k = src["k"][rows, :]
        vb = src["v"][rows, :]
        v = vb.astype(F32)
        expo = _dot(sums_ref[...], jnp.concatenate([src["lfh"][rows, :], src["lfl"][rows, :]], axis=0))
        g = expo[len(FINE_HALVES) * CHUNK:, :]

        diag = head_sums(q * k)
        pair = head_sums(jnp.where(odd, q * (1.0 - k) * prev_row(k), 0.0))
        v_prev = prev_row(v)
        yield

        slabs = [[None] * n_slabs for _ in range(HEADS)]

        def add(h, j, val):
            slabs[h][j] = val if slabs[h][j] is None else slabs[h][j] + val

        for li, b in enumerate(LEVEL_HALVES):
            m = masks_ref[li]
            if b in FINE_HALVES:
                qk = jnp.where((row % (2 * b)) >= b, q, k)
                x = (qk * jnp.exp2(expo[li * CHUNK:(li + 1) * CHUNK, :])).astype(BF16)
            else:
                pieces = []
                for r0 in range(0, CHUNK, b):
                    r = (r0 // (2 * b)) * 2 * b + b - 1
                    if (r0 // b) % 2:
                        pieces.append(q[r0:r0 + b] * jnp.exp2(g[r0:r0 + b] - g[r:r + 1]))
                    else:
                        pieces.append(k[r0:r0 + b] * jnp.exp2(g[r:r + 1] - g[r0:r0 + b]))
                x = jnp.concatenate(pieces, axis=0).astype(BF16)
            if b % SLAB == 0:
                upper = [j for j in range(n_slabs) if (j * SLAB // b) % 2]
                lhs = jnp.concatenate([x[j * SLAB:(j + 1) * SLAB] for j in upper], axis=0)
                ml = jnp.concatenate([m[j * SLAB:(j + 1) * SLAB] for j in upper], axis=0)
            else:
                upper, lhs, ml = list(range(n_slabs)), x, m
            for h, hc in enumerate(head_cols):
                s = ml * _dot_nt(lhs[:, hc], x[:, hc])
                for i, j in enumerate(upper):
                    add(h, j, s[i * SLAB:(i + 1) * SLAB])
            yield
        q_state = (q * jnp.exp2(g)).astype(BF16)
        k_end = k * jnp.exp2(g[CHUNK - 1:CHUNK] - g)
        chunk_decay = jnp.exp2(g[CHUNK - 1:CHUNK])
        o_intra, k_end_t, decay_col = [], [], []
        for h, hc in enumerate(head_cols):
            scores = jnp.concatenate(slabs[h], axis=0)
            o_intra.append(_dot(scores.astype(BF16), vb[:, hc])
                           + diag[h] * v[:, hc] + pair[h] * v_prev[:, hc])
            k_end_t.append(k_end[:, hc].T.astype(BF16))
            decay_col.append(jnp.broadcast_to(chunk_decay[:, hc], (8, HEAD_DIM)).T[:, 0:1])
        yield
        return o_intra, q_state, k_end_t, decay_col, vb

    def inter(c, o_intra, q_state, k_end_t, decay_col, vb):
        rows = slice(c * CHUNK, (c + 1) * CHUNK)
        states = [st_ref[h] for h in range(HEADS)]
        if c == 0:
            states = [jnp.where(new_sequence, 0.0, s) for s in states]
        for h, hc in enumerate(head_cols):
            o_dst[rows, hc] = (o_intra[h] + _dot(q_state[:, hc], states[h].astype(BF16))).astype(BF16)
        for h, hc in enumerate(head_cols):
            st_ref[h] = decay_col[h] * states[h] + _dot(k_end_t[h], vb[:, hc])

    pending = yield from intra(0)
    for c in range(n_chunks):
        nxt = (yield from intra(c + 1)) if c + 1 < n_chunks else None
        inter(c, *pending)
        yield
        pending = nxt


HGRN_PIECES = 1 + (TILE // CHUNK) * (len(LEVEL_HALVES) + 3)


def _merge_pipe(x_ref, rows, o_src, src, onorm_ref, wa_ref, pw_ref, ps_ref, wb_ref, wo_ref, out_ref,
                ext_ref, halo_ref, new_sequence, pos0):
    col_pieces = [slice(lo, lo + MERGE_COLS) for lo in range(0, D_MODEL, MERGE_COLS)]
    o = o_src[...].astype(F32)
    heads = []
    for h in range(HEADS):
        oh = o[:, h * HEAD_DIM:(h + 1) * HEAD_DIM]
        heads.append(oh * lax.rsqrt(jnp.mean(oh * oh, axis=-1, keepdims=True) + EPS))
        if h % 2:
            yield
    on = (jnp.concatenate(heads, axis=-1) * onorm_ref[...] * src["og"][...]).astype(BF16)
    ya = []
    for c in col_pieces:
        ya.append(_dot(on, wa_ref[:, c]))
        yield

    u = src["u"][...].astype(F32)
    ext_ref[0:POOL_HALO, :] = jnp.where(new_sequence, 0.0, halo_ref[...])
    ext_ref[POOL_HALO:POOL_HALO + TILE, :] = u
    halo_ref[...] = u[TILE - POOL_HALO:TILE, :]
    pos = (pos0 + 1 + lax.broadcasted_iota(jnp.int32, (TILE, POOL_CH), 0)).astype(F32)
    groups = []
    for g, w in enumerate(POOL_WINDOWS):
        cols = slice(g * POOL_CH, (g + 1) * POOL_CH)
        tok = ext_ref[POOL_HALO:POOL_HALO + TILE, cols]
        win = tok
        for j in range(1, w):
            win = win + ext_ref[POOL_HALO - j:POOL_HALO - j + TILE, cols]
        pooled = win / jnp.minimum(pos, float(w)) - tok
        groups.append(_dot(pooled.astype(BF16), pw_ref[g]))
        yield
    mixed = (jnp.concatenate(groups, axis=-1) * ps_ref[...]).astype(BF16)
    y = []
    for i, c in enumerate(col_pieces):
        yb = _dot(mixed, wb_ref[:, c])
        y.append((src["ga"][:, c] * ya[i] + src["gb"][:, c] * yb).astype(BF16))
        yield
    y = jnp.concatenate(y, axis=-1)
    for c in col_pieces:
        out_ref[rows, c] = x_ref[rows, c] + _dot(y, wo_ref[:, c])
        yield


MERGE_PIECES = HEADS // 2 + len(POOL_WINDOWS) + 3 * (D_MODEL // MERGE_COLS)


def _interleave(pipes):
    done = [0] * len(pipes)
    alive = [True] * len(pipes)
    while any(alive):
        i = min((i for i in range(len(pipes)) if alive[i]), key=lambda i: (done[i] + 1) / pipes[i][1])
        try:
            next(pipes[i][0])
            done[i] += 1
        except StopIteration:
            alive[i] = False


def _mixer_kernel(xp_ref, xm_ref, g_ref, w_ref, lb_ref, sums_ref, masks_ref, onorm_ref, wa_ref, pw_ref, ps_ref,
                  wb_ref, wo_ref, *rest, tiles_per_seq, n_cast):
    out_ref = rest[n_cast]
    _cast_slabs(rest[:n_cast], rest[n_cast + 1:2 * n_cast + 1])
    scratch = list(rest[2 * n_cast + 1:])

    def take(names):
        return {n: scratch.pop(0) for n in names}

    proj_buf = [take(PROJ_F32 + PROJ_BF16 + ("u",)) for _ in range(2)]
    merge_buf = [take(MERGE_NAMES) for _ in range(2)]
    o_buf = [scratch.pop(0) for _ in range(2)]
    st_ref, ext_ref, halo_ref = scratch
    i = pl.program_id(0)

    @pl.when(i == 0)
    def _():
        for ref in list(proj_buf[1].values()) + list(merge_buf[0].values()) + [o_buf[0], st_ref, halo_ref]:
            ref[...] = jnp.zeros_like(ref)

    for a in range(2):
        g = 2 * i + a
        rows = slice(a * TILE, (a + 1) * TILE)
        seq_h = lax.rem(g - 1 + tiles_per_seq, tiles_per_seq)
        seq_m = lax.rem(g - 2 + tiles_per_seq, tiles_per_seq)
        _interleave([
            (_proj_pipe(xp_ref[rows, :], g_ref, w_ref, lb_ref, proj_buf[a]), PROJ_PIECES),
            (_hgrn_pipe(proj_buf[1 - a], o_buf[1 - a], merge_buf[1 - a], sums_ref, masks_ref, st_ref,
                        seq_h == 0), HGRN_PIECES),
            (_merge_pipe(xm_ref, rows, o_buf[a], merge_buf[a], onorm_ref, wa_ref, pw_ref, ps_ref, wb_ref, wo_ref,
                         out_ref, ext_ref, halo_ref, seq_m == 0, seq_m * TILE), MERGE_PIECES),
        ])


def _mixer(x, g, w_in, hgrn_lb, onorm, wa, pw, ps, wb, wo, seq, cast=()):
    t = x.shape[0]
    n_steps = t // PAIR + 1
    cast_in, cast_out, cast_shapes = _cast_specs(cast, n_steps)
    sums, masks = _level_tables()
    sums2 = jnp.asarray(np.concatenate([sums, sums], axis=1), BF16)
    tile = lambda width, dtype: pltpu.VMEM((TILE, width), dtype)
    proj_slot = ([tile(HG_WIDTH, F32)] * len(PROJ_F32) + [tile(HG_WIDTH, BF16)] * len(PROJ_BF16)
                 + [tile(POOL_WIDTH, BF16)])
    merge_slot = [tile(HG_WIDTH, BF16)] * 3 + [tile(POOL_WIDTH, BF16)]
    return pl.pallas_call(
        functools.partial(_mixer_kernel, tiles_per_seq=seq // TILE, n_cast=len(cast)),
        grid=(n_steps,),
        in_specs=[pl.BlockSpec((PAIR, D_MODEL), lambda i: (jnp.minimum(i, n_steps - 2), 0)),
                  pl.BlockSpec((PAIR, D_MODEL), lambda i: (jnp.maximum(i - 1, 0), 0)),
                  _const_spec((1, D_MODEL)), _const_spec((D_MODEL, IN_SPLITS[-1])), _const_spec(hgrn_lb.shape),
                  _const_spec(sums2.shape), _const_spec(masks.shape),
                  _const_spec((1, HG_WIDTH)), _const_spec((HG_WIDTH, D_MODEL)),
                  _const_spec((len(POOL_WINDOWS), POOL_CH, POOL_CH)), _const_spec((1, POOL_WIDTH)),
                  _const_spec((POOL_WIDTH, D_MODEL)), _const_spec((D_MODEL, D_MODEL))] + cast_in,
        out_specs=[pl.BlockSpec((PAIR, D_MODEL), lambda i: (jnp.maximum(i - 1, 0), 0))] + cast_out,
        out_shape=[jax.ShapeDtypeStruct((t, D_MODEL), F32)] + cast_shapes,
        scratch_shapes=(proj_slot * 2 + merge_slot * 2 + [tile(HG_WIDTH, BF16)] * 2
                        + [pltpu.VMEM((HEADS, HEAD_DIM, HEAD_DIM), F32),
                           pltpu.VMEM((POOL_HALO + TILE, POOL_WIDTH), F32),
                           pltpu.VMEM((POOL_HALO, POOL_WIDTH), F32)]),
        compiler_params=_params(1, sequential=True),
        name="mixer",
    )(x, x, g, w_in, hgrn_lb, sums2, jnp.asarray(masks), onorm, wa, pw, ps, wb, wo, *cast)


def _ffn_ple_kernel(x_ref, p_ref, g_ref, w1_ref, w3_ref, w2_ref, gn_ref, wg_ref, wp_ref, pn_ref, fn_ref, o_ref):
    x = x_ref[...]
    e = _rms(_dot(p_ref[...].astype(BF16), wp_ref[...]), pn_ref[...])
    h = _rms(x, g_ref[...]).astype(BF16)
    act = (_silu(_dot(h, w1_ref[...])) * _dot(h, w3_ref[...])).astype(BF16)
    x = x + 0.5 * _dot(act, w2_ref[...])
    gate = jax.nn.sigmoid(_dot(_rms(x, gn_ref[...]).astype(BF16), wg_ref[...]))
    o_ref[...] = _rms(x + gate * e, fn_ref[...])


def _ffn_ple(x, p, g, w1, w3, w2, gn, wg, wp, pn, fn, tm=512):
    t = x.shape[0]
    row = pl.BlockSpec((tm, D_MODEL), lambda i: (i, 0))
    vec = _const_spec((1, D_MODEL))
    return pl.pallas_call(
        _ffn_ple_kernel,
        grid=(t // tm,),
        in_specs=[row, pl.BlockSpec((tm, PLE_DIM), lambda i: (i, 0)), vec, _const_spec((D_MODEL, D_FF)),
                  _const_spec((D_MODEL, D_FF)), _const_spec((D_FF, D_MODEL)), vec,
                  _const_spec((D_MODEL, D_MODEL)), _const_spec((PLE_DIM, D_MODEL)), vec, vec],
        out_specs=row,
        out_shape=jax.ShapeDtypeStruct((t, D_MODEL), F32),
        compiler_params=_params(1),
        name="ffn_ple",
    )(x, p, g, w1, w3, w2, gn, wg, wp, pn, fn)


def kernel(x, p, ffn1_norm, ffn1_w1, ffn1_w3, ffn1_w2, mix_norm, w_in, hgrn_lb, hgrn_onorm, w_branch_a, pool_w, pool_scale, w_branch_b, w_out, ffn2_norm, ffn2_w1, ffn2_w3, ffn2_w2, ple_norm, ple_w_gate, ple_w_proj, ple_post_norm, final_norm):
    batch, seq, d = x.shape
    assert d == D_MODEL and p.shape[0] == 1 and hgrn_lb.shape == (2, HG_WIDTH)
    assert seq % PAIR == 0
    t = batch * seq
    bf = lambda w: w.astype(BF16)
    vec = lambda g: g.reshape(1, -1)

    xt = x.reshape(t, d)
    xt, w_in_b, wa, wo, wb, pw = _ffn(
        xt, vec(ffn1_norm[0]), bf(ffn1_w1[0]), bf(ffn1_w3[0]), bf(ffn1_w2[0]),
        cast=(w_in[0], w_branch_a[0], w_out[0], w_branch_b[0], pool_w[0].reshape(POOL_WIDTH, POOL_CH)))
    xt, w1, w3, w2, wg, wp = _mixer(
        xt, vec(mix_norm[0]), w_in_b, hgrn_lb, vec(hgrn_onorm[0]), wa, pw.reshape(pool_w.shape[1:]),
        vec(pool_scale[0]), wb, wo, seq, cast=(ffn2_w1[0], ffn2_w3[0], ffn2_w2[0], ple_w_gate[0], ple_w_proj[0]))
    out = _ffn_ple(xt, p[0].reshape(t, PLE_DIM), vec(ffn2_norm[0]), w1, w3, w2,
                   vec(ple_norm[0]), wg, wp, vec(ple_post_norm[0]), vec(final_norm))
    return out.reshape(batch, seq, d)
```

```python
import functools

import numpy as np
import jax
import jax.numpy as jnp
from jax import lax
from jax.experimental import pallas as pl
from jax.experimental.pallas import tpu as pltpu

D_MODEL = 1024
D_FF = 2816
PLE_DIM = 256
HEADS = 8
HEAD_DIM = 128
HG_WIDTH = HEADS * HEAD_DIM
POOL_WINDOWS = (2, 4, 8, 16)
POOL_CH = 128
POOL_WIDTH = len(POOL_WINDOWS) * POOL_CH
POOL_HALO = 16
IN_SPLITS = (0, 1024, 2048, 3072, 4096, 4608, 5632, 6656)
EPS = 1e-6
LOG2E = 1.4426950408889634

CHUNK = 64
FINE_HALVES = (2, 4)
LEVEL_HALVES = FINE_HALVES + (8, 16, 32)
SLAB = 16

TILE = 256
PAIR = 2 * TILE
PROJ_COLS = 256
MERGE_COLS = 256

V7X_VMEM_BYTES = 64 * 1024 * 1024
VMEM_LIMIT = V7X_VMEM_BYTES - 8 * 1024 * 1024

BF16 = jnp.bfloat16
F32 = jnp.float32


def _rms(x, g):
    return x * lax.rsqrt(jnp.mean(x * x, axis=-1, keepdims=True) + EPS) * g


def _silu(x):
    return x * jax.nn.sigmoid(x)


def _dot(a, b):
    return jnp.dot(a, b, preferred_element_type=F32)


def _dot_nt(a, b):
    return lax.dot_general(a, b, (((1,), (1,)), ((), ())), preferred_element_type=F32)


def _const_spec(shape):
    zeros = (0,) * len(shape)
    return pl.BlockSpec(shape, lambda *_: zeros, pipeline_mode=pl.Buffered(1))


def _params(n_grid_dims, sequential=False):
    sem = ("arbitrary" if sequential else "parallel",) * n_grid_dims
    return pltpu.CompilerParams(dimension_semantics=sem, vmem_limit_bytes=VMEM_LIMIT)


def _cast_specs(weights, n_steps):
    in_specs, out_specs, out_shapes = [], [], []
    for w in weights:
        rows, cols = w.shape
        slab = next(s for s in range(SLAB, rows + 1, SLAB) if rows % s == 0 and s * n_steps >= rows)
        last = rows // slab - 1
        index = lambda i, last=last: (jnp.minimum(i, last), 0)
        in_specs.append(pl.BlockSpec((slab, cols), index))
        out_specs.append(pl.BlockSpec((slab, cols), index))
        out_shapes.append(jax.ShapeDtypeStruct((rows, cols), BF16))
    return in_specs, out_specs, out_shapes


def _cast_slabs(src_refs, dst_refs):
    for src, dst in zip(src_refs, dst_refs):
        dst[...] = src[...].astype(BF16)


def _ffn_kernel(x_ref, g_ref, w1_ref, w3_ref, w2_ref, *rest):
    n_cast = len(rest) // 2
    o_ref = rest[n_cast]
    _cast_slabs(rest[:n_cast], rest[n_cast + 1:])
    x = x_ref[...]
    h = _rms(x, g_ref[...]).astype(BF16)
    a = _dot(h, w1_ref[...])
    b = _dot(h, w3_ref[...])
    act = (_silu(a) * b).astype(BF16)
    o_ref[...] = x + 0.5 * _dot(act, w2_ref[...])


def _ffn(x, g, w1, w3, w2, cast=(), tm=512):
    t = x.shape[0]
    row = pl.BlockSpec((tm, D_MODEL), lambda i: (i, 0))
    cast_in, cast_out, cast_shapes = _cast_specs(cast, t // tm)
    return pl.pallas_call(
        _ffn_kernel,
        grid=(t // tm,),
        in_specs=[row, _const_spec((1, D_MODEL)), _const_spec((D_MODEL, D_FF)),
                  _const_spec((D_MODEL, D_FF)), _const_spec((D_FF, D_MODEL))] + cast_in,
        out_specs=[row] + cast_out,
        out_shape=[jax.ShapeDtypeStruct((t, D_MODEL), F32)] + cast_shapes,
        compiler_params=_params(1),
        name="ffn",
    )(x, g, w1, w3, w2, *cast)


def _level_tables():
    c = CHUNK
    sums = np.zeros((len(FINE_HALVES) + 1, c, c), np.float32)
    masks = np.zeros((len(LEVEL_HALVES), c, c), np.float32)
    for li, b in enumerate(LEVEL_HALVES):
        for t in range(c):
            base = (t // (2 * b)) * 2 * b
            r = base + b - 1
            if t > r:
                masks[li, t, base:r + 1] = 1.0
            if b in FINE_HALVES:
                sums[li, t, (r + 1 if t > r else t + 1):(t + 1 if t > r else r + 1)] = 1.0
    for t in range(c):
        sums[len(FINE_HALVES), t, :t + 1] = 1.0
    return sums.reshape(-1, c), masks


PROJ_F32 = ("q", "k")
PROJ_BF16 = ("v", "lfh", "lfl", "og", "ga", "gb")
MERGE_NAMES = ("og", "ga", "gb", "u")


def _proj_pipe(x, g_ref, w_ref, lb_ref, dst):
    h = _rms(x, g_ref[...]).astype(BF16)
    lbp = lb_ref[...]
    e = jnp.exp(lbp - jnp.max(lbp, axis=0, keepdims=True))
    lb = e[0:1, :] / jnp.sum(e, axis=0, keepdims=True)
    yield

    def pieces(i):
        for lo in range(IN_SPLITS[i], IN_SPLITS[i + 1], PROJ_COLS):
            yield slice(lo - IN_SPLITS[i], lo - IN_SPLITS[i] + PROJ_COLS), _dot(h, w_ref[:, lo:lo + PROJ_COLS])

    for c, z in pieces(0):
        dst["q"][:, c] = _silu(z)
        yield
    for c, z in pieces(1):
        f = lb[:, c] + (1.0 - lb[:, c]) * jax.nn.sigmoid(z)
        lf = jnp.log(f) * LOG2E
        hi = lf.astype(BF16)
        dst["lfh"][:, c] = hi
        dst["lfl"][:, c] = (lf - hi.astype(F32)).astype(BF16)
        dst["k"][:, c] = 1.0 - f
        yield
    for c, z in pieces(2):
        dst["v"][:, c] = z.astype(BF16)
        yield
    for c, z in pieces(3):
        dst["og"][:, c] = _silu(z).astype(BF16)
        yield
    for c, z in pieces(4):
        dst["u"][:, c] = z.astype(BF16)
        yield
    for name, i in (("ga", 5), ("gb", 6)):
        for c, z in pieces(i):
            dst[name][:, c] = jax.nn.sigmoid(z).astype(BF16)
            yield


PROJ_PIECES = 1 + IN_SPLITS[-1] // PROJ_COLS


def _hgrn_pipe(src, o_dst, merge_dst, sums_ref, masks_ref, st_ref, new_sequence):
    for name in MERGE_NAMES:
        merge_dst[name][...] = src[name][...]
    yield

    row = lax.broadcasted_iota(jnp.int32, (CHUNK, 1), 0)
    odd = (row % 2) == 1
    head_cols = [slice(h * HEAD_DIM, (h + 1) * HEAD_DIM) for h in range(HEADS)]
    n_slabs = CHUNK // SLAB
    n_chunks = TILE // CHUNK

    def head_sums(x):
        return [jnp.sum(x[:, hc], axis=-1, keepdims=True) for hc in head_cols]

    def prev_row(x):
        return jnp.concatenate([pltpu.roll(x[r:r + 8], 1, 0) for r in range(0, CHUNK, 8)], axis=0)

    def intra(c):
        rows = slice(c * CHUNK, (c + 1) * CHUNK)
        q = src["q"][rows, :]
        k = src["k"][rows, :]
        vb = src["v"][rows, :]
        v = vb.astype(F32)
        expo = _dot(sums_ref[...], jnp.concatenate([src["lfh"][rows, :], src["lfl"][rows, :]], axis=0))
        g = expo[len(FINE_HALVES) * CHUNK:, :]

        diag = head_sums(q * k)
        pair = head_sums(jnp.where(odd, q * (1.0 - k) * prev_row(k), 0.0))
        v_prev = prev_row(v)
        yield

        slabs = [[None] * n_slabs for _ in range(HEADS)]

        def add(h, j, val):
            slabs[h][j] = val if slabs[h][j] is None else slabs[h][j] + val

        for li, b in enumerate(LEVEL_HALVES):
            m = masks_ref[li]
            if b in FINE_HALVES:
                qk = jnp.where((row % (2 * b)) >= b, q, k)
                x = (qk * jnp.exp2(expo[li * CHUNK:(li + 1) * CHUNK, :])).astype(BF16)
            else:
                pieces = []
                for r0 in range(0, CHUNK, b):
                    r = (r0 // (2 * b)) * 2 * b + b - 1
                    if (r0 // b) % 2:
                        pieces.append(q[r0:r0 + b] * jnp.exp2(g[r0:r0 + b] - g[r:r + 1]))
                    else:
                        pieces.append(k[r0:r0 + b] * jnp.exp2(g[r:r + 1] - g[r0:r0 + b]))
                x = jnp.concatenate(pieces, axis=0).astype(BF16)
            if b % SLAB == 0:
                upper = [j for j in range(n_slabs) if (j * SLAB // b) % 2]
                lhs = jnp.concatenate([x[j * SLAB:(j + 1) * SLAB] for j in upper], axis=0)
                ml = jnp.concatenate([m[j * SLAB:(j + 1) * SLAB] for j in upper], axis=0)
            else:
                upper, lhs, ml = list(range(n_slabs)), x, m
            for h, hc in enumerate(head_cols):
                s = ml * _dot_nt(lhs[:, hc], x[:, hc])
                for i, j in enumerate(upper):
                    add(h, j, s[i * SLAB:(i + 1) * SLAB])
            yield
        q_state = (q * jnp.exp2(g)).astype(BF16)
        k_end = k * jnp.exp2(g[CHUNK - 1:CHUNK] - g)
        chunk_decay = jnp.exp2(g[CHUNK - 1:CHUNK])
        o_intra, k_end_t, decay_col = [], [], []
        for h, hc in enumerate(head_cols):
            scores = jnp.concatenate(slabs[h], axis=0)
            o_intra.append(_dot(scores.astype(BF16), vb[:, hc])
                           + diag[h] * v[:, hc] + pair[h] * v_prev[:, hc])
            k_end_t.append(k_end[:, hc].T.astype(BF16))
            decay_col.append(jnp.broadcast_to(chunk_decay[:, hc], (8, HEAD_DIM)).T[:, 0:1])
        yield
        return o_intra, q_state, k_end_t, decay_col, vb

    def inter(c, o_intra, q_state, k_end_t, decay_col, vb):
        rows = slice(c * CHUNK, (c + 1) * CHUNK)
        states = [st_ref[h] for h in range(HEADS)]
        if c == 0:
            states = [jnp.where(new_sequence, 0.0, s) for s in states]
        for h, hc in enumerate(head_cols):
            o_dst[rows, hc] = (o_intra[h] + _dot(q_state[:, hc], states[h].astype(BF16))).astype(BF16)
        for h, hc in enumerate(head_cols):
            st_ref[h] = decay_col[h] * states[h] + _dot(k_end_t[h], vb[:, hc])

    pending = yield from intra(0)
    for c in range(n_chunks):
        nxt = (yield from intra(c + 1)) if c + 1 < n_chunks else None
        inter(c, *pending)
        yield
        pending = nxt


HGRN_PIECES = 1 + (TILE // CHUNK) * (len(LEVEL_HALVES) + 3)


def _merge_pipe(x_ref, rows, o_src, src, onorm_ref, wa_ref, pw_ref, ps_ref, wb_ref, wo_ref, out_ref,
                ext_ref, halo_ref, new_sequence, pos0):
    col_pieces = [slice(lo, lo + MERGE_COLS) for lo in range(0, D_MODEL, MERGE_COLS)]
    o = o_src[...].astype(F32)
    heads = []
    for h in range(HEADS):
        oh = o[:, h * HEAD_DIM:(h + 1) * HEAD_DIM]
        heads.append(oh * lax.rsqrt(jnp.mean(oh * oh, axis=-1, keepdims=True) + EPS))
        if h % 2:
            yield
    on = (jnp.concatenate(heads, axis=-1) * onorm_ref[...] * src["og"][...]).astype(BF16)
    ya = []
    for c in col_pieces:
        ya.append(_dot(on, wa_ref[:, c]))
        yield

    u = src["u"][...].astype(F32)
    ext_ref[0:POOL_HALO, :] = jnp.where(new_sequence, 0.0, halo_ref[...])
    ext_ref[POOL_HALO:POOL_HALO + TILE, :] = u
    halo_ref[...] = u[TILE - POOL_HALO:TILE, :]
    pos = (pos0 + 1 + lax.broadcasted_iota(jnp.int32, (TILE, POOL_CH), 0)).astype(F32)
    groups = []
    for g, w in enumerate(POOL_WINDOWS):
        cols = slice(g * POOL_CH, (g + 1) * POOL_CH)
        tok = ext_ref[POOL_HALO:POOL_HALO + TILE, cols]
        win = tok
        for j in range(1, w):
            win = win + ext_ref[POOL_HALO - j:POOL_HALO - j + TILE, cols]
        pooled = win / jnp.minimum(pos, float(w)) - tok
        groups.append(_dot(pooled.astype(BF16), pw_ref[g]))
        yield
    mixed = (jnp.concatenate(groups, axis=-1) * ps_ref[...]).astype(BF16)
    y = []
    for i, c in enumerate(col_pieces):
        yb = _dot(mixed, wb_ref[:, c])
        y.append((src["ga"][:, c] * ya[i] + src["gb"][:, c] * yb).astype(BF16))
        yield
    y = jnp.concatenate(y, axis=-1)
    for c in col_pieces:
        out_ref[rows, c] = x_ref[rows, c] + _dot(y, wo_ref[:, c])
        yield


MERGE_PIECES = HEADS // 2 + len(POOL_WINDOWS) + 3 * (D_MODEL // MERGE_COLS)


def _interleave(pipes):
    done = [0] * len(pipes)
    alive = [True] * len(pipes)
    while any(alive):
        i = min((i for i in range(len(pipes)) if alive[i]), key=lambda i: (done[i] + 1) / pipes[i][1])
        try:
            next(pipes[i][0])
            done[i] += 1
        except StopIteration:
            alive[i] = False


def _mixer_kernel(xp_ref, xm_ref, g_ref, w_ref, lb_ref, sums_ref, masks_ref, onorm_ref, wa_ref, pw_ref, ps_ref,
                  wb_ref, wo_ref, *rest, tiles_per_seq, n_cast):
    out_ref = rest[n_cast]
    _cast_slabs(rest[:n_cast], rest[n_cast + 1:2 * n_cast + 1])
    scratch = list(rest[2 * n_cast + 1:])

    def take(names):
        return {n: scratch.pop(0) for n in names}

    proj_buf = [take(PROJ_F32 + PROJ_BF16 + ("u",)) for _ in range(2)]
    merge_buf = [take(MERGE_NAMES) for _ in range(2)]
    o_buf = [scratch.pop(0) for _ in range(2)]
    st_ref, ext_ref, halo_ref = scratch
    i = pl.program_id(0)

    @pl.when(i == 0)
    def _():
        for ref in list(proj_buf[1].values()) + list(merge_buf[0].values()) + [o_buf[0], st_ref, halo_ref]:
            ref[...] = jnp.zeros_like(ref)

    for a in range(2):
        g = 2 * i + a
        rows = slice(a * TILE, (a + 1) * TILE)
        seq_h = lax.rem(g - 1 + tiles_per_seq, tiles_per_seq)
        seq_m = lax.rem(g - 2 + tiles_per_seq, tiles_per_seq)
        _interleave([
            (_proj_pipe(xp_ref[rows, :], g_ref, w_ref, lb_ref, proj_buf[a]), PROJ_PIECES),
            (_hgrn_pipe(proj_buf[1 - a], o_buf[1 - a], merge_buf[1 - a], sums_ref, masks_ref, st_ref,
                        seq_h == 0), HGRN_PIECES),
            (_merge_pipe(xm_ref, rows, o_buf[a], merge_buf[a], onorm_ref, wa_ref, pw_ref, ps_ref, wb_ref, wo_ref,
                         out_ref, ext_ref, halo_ref, seq_m == 0, seq_m * TILE), MERGE_PIECES),
        ])


def _mixer(x, g, w_in, hgrn_lb, onorm, wa, pw, ps, wb, wo, seq, cast=()):
    t = x.shape[0]
    n_steps = t // PAIR + 1
    cast_in, cast_out, cast_shapes = _cast_specs(cast, n_steps)
    sums, masks = _level_tables()
    sums2 = jnp.asarray(np.concatenate([sums, sums], axis=1), BF16)
    tile = lambda width, dtype: pltpu.VMEM((TILE, width), dtype)
    proj_slot = ([tile(HG_WIDTH, F32)] * len(PROJ_F32) + [tile(HG_WIDTH, BF16)] * len(PROJ_BF16)
                 + [tile(POOL_WIDTH, BF16)])
    merge_slot = [tile(HG_WIDTH, BF16)] * 3 + [tile(POOL_WIDTH, BF16)]
    return pl.pallas_call(
        functools.partial(_mixer_kernel, tiles_per_seq=seq // TILE, n_cast=len(cast)),
        grid=(n_steps,),
        in_specs=[pl.BlockSpec((PAIR, D_MODEL), lambda i: (jnp.minimum(i, n_steps - 2), 0)),
                  pl.BlockSpec((PAIR, D_MODEL), lambda i: (jnp.maximum(i - 1, 0), 0)),
                  _const_spec((1, D_MODEL)), _const_spec((D_MODEL, IN_SPLITS[-1])), _const_spec(hgrn_lb.shape),
                  _const_spec(sums2.shape), _const_spec(masks.shape),
                  _const_spec((1, HG_WIDTH)), _const_spec((HG_WIDTH, D_MODEL)),
                  _const_spec((len(POOL_WINDOWS), POOL_CH, POOL_CH)), _const_spec((1, POOL_WIDTH)),
                  _const_spec((POOL_WIDTH, D_MODEL)), _const_spec((D_MODEL, D_MODEL))] + cast_in,
        out_specs=[pl.BlockSpec((PAIR, D_MODEL), lambda i: (jnp.maximum(i - 1, 0), 0))] + cast_out,
        out_shape=[jax.ShapeDtypeStruct((t, D_MODEL), F32)] + cast_shapes,
        scratch_shapes=(proj_slot * 2 + merge_slot * 2 + [tile(HG_WIDTH, BF16)] * 2
                        + [pltpu.VMEM((HEADS, HEAD_DIM, HEAD_DIM), F32),
                           pltpu.VMEM((POOL_HALO + TILE, POOL_WIDTH), F32),
                           pltpu.VMEM((POOL_HALO, POOL_WIDTH), F32)]),
        compiler_params=_params(1, sequential=True),
        name="mixer",
    )(x, x, g, w_in, hgrn_lb, sums2, jnp.asarray(masks), onorm, wa, pw, ps, wb, wo, *cast)


def _ffn_ple_kernel(x_ref, p_ref, g_ref, w1_ref, w3_ref, w2_ref, gn_ref, wg_ref, wp_ref, pn_ref, fn_ref, o_ref):
    x = x_ref[...]
    e = _rms(_dot(p_ref[...].astype(BF16), wp_ref[...]), pn_ref[...])
    h = _rms(x, g_ref[...]).astype(BF16)
    act = (_silu(_dot(h, w1_ref[...])) * _dot(h, w3_ref[...])).astype(BF16)
    half = x.shape[0] // 2
    halves = (slice(0, half), slice(half, 2 * half))
    xs = [x[r] + 0.5 * _dot(act[r], w2_ref[...]) for r in halves]
    for r, xh in zip(halves, xs):
        gate = jax.nn.sigmoid(_dot(_rms(xh, gn_ref[...]).astype(BF16), wg_ref[...]))
        o_ref[r, :] = _rms(xh + gate * e[r], fn_ref[...])


def _ffn_ple(x, p, g, w1, w3, w2, gn, wg, wp, pn, fn, tm=512):
    t = x.shape[0]
    row = pl.BlockSpec((tm, D_MODEL), lambda i: (i, 0))
    vec = _const_spec((1, D_MODEL))
    return pl.pallas_call(
        _ffn_ple_kernel,
        grid=(t // tm,),
        in_specs=[row, pl.BlockSpec((tm, PLE_DIM), lambda i: (i, 0)), vec, _const_spec((D_MODEL, D_FF)),
                  _const_spec((D_MODEL, D_FF)), _const_spec((D_FF, D_MODEL)), vec,
                  _const_spec((D_MODEL, D_MODEL)), _const_spec((PLE_DIM, D_MODEL)), vec, vec],
        out_specs=row,
        out_shape=jax.ShapeDtypeStruct((t, D_MODEL), F32),
        compiler_params=_params(1),
        name="ffn_ple",
    )(x, p, g, w1, w3, w2, gn, wg, wp, pn, fn)


def kernel(x, p, ffn1_norm, ffn1_w1, ffn1_w3, ffn1_w2, mix_norm, w_in, hgrn_lb, hgrn_onorm, w_branch_a, pool_w, pool_scale, w_branch_b, w_out, ffn2_norm, ffn2_w1, ffn2_w3, ffn2_w2, ple_norm, ple_w_gate, ple_w_proj, ple_post_norm, final_norm):
    batch, seq, d = x.shape
    assert d == D_MODEL and p.shape[0] == 1 and hgrn_lb.shape == (2, HG_WIDTH)
    assert seq % PAIR == 0
    t = batch * seq
    bf = lambda w: w.astype(BF16)
    vec = lambda g: g.reshape(1, -1)

    xt = x.reshape(t, d)
    xt, w_in_b, wa, wo, wb, pw = _ffn(
        xt, vec(ffn1_norm[0]), bf(ffn1_w1[0]), bf(ffn1_w3[0]), bf(ffn1_w2[0]),
        cast=(w_in[0], w_branch_a[0], w_out[0], w_branch_b[0], pool_w[0].reshape(POOL_WIDTH, POOL_CH)))
    xt, w1, w3, w2, wg, wp = _mixer(
        xt, vec(mix_norm[0]), w_in_b, hgrn_lb, vec(hgrn_onorm[0]), wa, pw.reshape(pool_w.shape[1:]),
        vec(pool_scale[0]), wb, wo, seq, cast=(ffn2_w1[0], ffn2_w3[0], ffn2_w2[0], ple_w_gate[0], ple_w_proj[0]))
    out = _ffn_ple(xt, p[0].reshape(t, PLE_DIM), vec(ffn2_norm[0]), w1, w3, w2,
                   vec(ple_norm[0]), wg, wp, vec(ple_post_norm[0]), vec(final_norm))
    return out.reshape(batch, seq, d)
```

```python
import functools

import numpy as np
import jax
import jax.numpy as jnp
from jax import lax
from jax.experimental import pallas as pl
from jax.experimental.pallas import tpu as pltpu

D_MODEL = 1024
D_FF = 2816
PLE_DIM = 256
HEADS = 8
HEAD_DIM = 128
HG_WIDTH = HEADS * HEAD_DIM
POOL_WINDOWS = (2, 4, 8, 16)
POOL_CH = 128
POOL_WIDTH = len(POOL_WINDOWS) * POOL_CH
POOL_HALO = 16
IN_SPLITS = (0, 1024, 2048, 3072, 4096, 4608, 5632, 6656)
EPS = 1e-6
LOG2E = 1.4426950408889634

CHUNK = 64
FINE_HALVES = (2, 4)
LEVEL_HALVES = FINE_HALVES + (8, 16, 32)
SLAB = 16
SUBLANES = 8
DIAG_MASK = len(LEVEL_HALVES)
PAIR_MASK = DIAG_MASK + 1

TILE = 256
PAIR = 2 * TILE
PROJ_COLS = 256
MERGE_COLS = 256

V7X_VMEM_BYTES = 64 * 1024 * 1024
VMEM_LIMIT = V7X_VMEM_BYTES - 8 * 1024 * 1024

BF16 = jnp.bfloat16
F32 = jnp.float32


def _rms(x, g):
    return x * lax.rsqrt(jnp.mean(x * x, axis=-1, keepdims=True) + EPS) * g


def _silu(x):
    return x * jax.nn.sigmoid(x)


def _dot(a, b):
    return jnp.dot(a, b, preferred_element_type=F32)


def _dot_nt(a, b):
    return lax.dot_general(a, b, (((1,), (1,)), ((), ())), preferred_element_type=F32)


def _const_spec(shape):
    zeros = (0,) * len(shape)
    return pl.BlockSpec(shape, lambda *_: zeros, pipeline_mode=pl.Buffered(1))


def _params(n_grid_dims, sequential=False):
    sem = ("arbitrary" if sequential else "parallel",) * n_grid_dims
    return pltpu.CompilerParams(dimension_semantics=sem, vmem_limit_bytes=VMEM_LIMIT)


def _cast_specs(weights, n_steps):
    in_specs, out_specs, out_shapes = [], [], []
    for w in weights:
        rows, cols = w.shape
        slab = next(s for s in range(SLAB, rows + 1, SLAB) if rows % s == 0 and s * n_steps >= rows)
        last = rows // slab - 1
        index = lambda i, last=last: (jnp.minimum(i, last), 0)
        in_specs.append(pl.BlockSpec((slab, cols), index))
        out_specs.append(pl.BlockSpec((slab, cols), index))
        out_shapes.append(jax.ShapeDtypeStruct((rows, cols), BF16))
    return in_specs, out_specs, out_shapes


def _cast_slabs(src_refs, dst_refs):
    for src, dst in zip(src_refs, dst_refs):
        dst[...] = src[...].astype(BF16)


def _ffn_kernel(x_ref, g_ref, w1_ref, w3_ref, w2_ref, *rest):
    n_cast = len(rest) // 2
    o_ref = rest[n_cast]
    _cast_slabs(rest[:n_cast], rest[n_cast + 1:])
    x = x_ref[...]
    h = _rms(x, g_ref[...]).astype(BF16)
    a = _dot(h, w1_ref[...])
    b = _dot(h, w3_ref[...])
    act = (_silu(a) * b).astype(BF16)
    o_ref[...] = x + 0.5 * _dot(act, w2_ref[...])


def _ffn(x, g, w1, w3, w2, cast=(), tm=512):
    t = x.shape[0]
    row = pl.BlockSpec((tm, D_MODEL), lambda i: (i, 0))
    cast_in, cast_out, cast_shapes = _cast_specs(cast, t // tm)
    return pl.pallas_call(
        _ffn_kernel,
        grid=(t // tm,),
        in_specs=[row, _const_spec((1, D_MODEL)), _const_spec((D_MODEL, D_FF)),
                  _const_spec((D_MODEL, D_FF)), _const_spec((D_FF, D_MODEL))] + cast_in,
        out_specs=[row] + cast_out,
        out_shape=[jax.ShapeDtypeStruct((t, D_MODEL), F32)] + cast_shapes,
        compiler_params=_params(1),
        name="ffn",
    )(x, g, w1, w3, w2, *cast)


def _level_tables():
    c = CHUNK
    sums = np.zeros((len(FINE_HALVES) + 1, c, c), np.float32)
    masks = np.zeros((len(LEVEL_HALVES) + 2, c, c), np.float32)
    masks[DIAG_MASK] = np.eye(c)
    masks[PAIR_MASK, 1::2] = np.eye(c)[0::2]
    for li, b in enumerate(LEVEL_HALVES):
        for t in range(c):
            base = (t // (2 * b)) * 2 * b
            r = base + b - 1
            if t > r:
                masks[li, t, base:r + 1] = 1.0
            if b in FINE_HALVES:
                sums[li, t, (r + 1 if t > r else t + 1):(t + 1 if t > r else r + 1)] = 1.0
    for t in range(c):
        sums[len(FINE_HALVES), t, :t + 1] = 1.0
    return sums.reshape(-1, c), masks


PROJ_F32 = ("q", "k")
PROJ_BF16 = ("v", "lfh", "lfl", "og", "ga", "gb")
MERGE_NAMES = ("og", "ga", "gb", "u")


def _proj_pipe(x, g_ref, w_ref, lb_ref, dst):
    h = _rms(x, g_ref[...]).astype(BF16)
    lbp = lb_ref[...]
    e = jnp.exp(lbp - jnp.max(lbp, axis=0, keepdims=True))
    lb = e[0:1, :] / jnp.sum(e, axis=0, keepdims=True)
    yield

    def pieces(i):
        for lo in range(IN_SPLITS[i], IN_SPLITS[i + 1], PROJ_COLS):
            yield slice(lo - IN_SPLITS[i], lo - IN_SPLITS[i] + PROJ_COLS), _dot(h, w_ref[:, lo:lo + PROJ_COLS])

    for c, z in pieces(0):
        dst["q"][:, c] = _silu(z)
        yield
    for c, z in pieces(1):
        f = lb[:, c] + (1.0 - lb[:, c]) * jax.nn.sigmoid(z)
        lf = jnp.log(f) * LOG2E
        hi = lf.astype(BF16)
        dst["lfh"][:, c] = hi
        dst["lfl"][:, c] = (lf - hi.astype(F32)).astype(BF16)
        dst["k"][:, c] = 1.0 - f
        yield
    for c, z in pieces(2):
        dst["v"][:, c] = z.astype(BF16)
        yield
    for c, z in pieces(3):
        dst["og"][:, c] = _silu(z).astype(BF16)
        yield
    for c, z in pieces(4):
        dst["u"][:, c] = z.astype(BF16)
        yield
    for name, i in (("ga", 5), ("gb", 6)):
        for c, z in pieces(i):
            dst[name][:, c] = jax.nn.sigmoid(z).astype(BF16)
            yield


PROJ_PIECES = 1 + IN_SPLITS[-1] // PROJ_COLS


def _hgrn_pipe(src, o_dst, merge_dst, sums_ref, masks_ref, st_ref, new_sequence):
    for name in MERGE_NAMES:
        merge_dst[name][...] = src[name][...]
    yield

    row = lax.broadcasted_iota(jnp.int32, (CHUNK, 1), 0)
    head_cols = [slice(h * HEAD_DIM, (h + 1) * HEAD_DIM) for h in range(HEADS)]
    n_slabs = CHUNK // SLAB
    n_chunks = TILE // CHUNK

    def head_sums(x):
        return [jnp.sum(x[:, hc], axis=-1, keepdims=True) for hc in head_cols]

    def prev_row(x):
        return jnp.concatenate([pltpu.roll(x[r:r + SUBLANES], 1, 0) for r in range(0, CHUNK, SUBLANES)], axis=0)

    def intra(c):
        rows = slice(c * CHUNK, (c + 1) * CHUNK)
        q = src["q"][rows, :]
        k = src["k"][rows, :]
        vb = src["v"][rows, :]
        expo = _dot(sums_ref[...], jnp.concatenate([src["lfh"][rows, :], src["lfl"][rows, :]], axis=0))
        g = expo[len(FINE_HALVES) * CHUNK:, :]

        diag = head_sums(q * k)
        pair = head_sums(q * (1.0 - k) * prev_row(k))
        yield

        slabs = [[None] * n_slabs for _ in range(HEADS)]

        def add(h, j, val):
            slabs[h][j] = val if slabs[h][j] is None else slabs[h][j] + val

        for li, b in enumerate(LEVEL_HALVES):
            m = masks_ref[li]
            if b in FINE_HALVES:
                qk = jnp.where((row % (2 * b)) >= b, q, k)
                x = (qk * jnp.exp2(expo[li * CHUNK:(li + 1) * CHUNK, :])).astype(BF16)
            else:
                pieces = []
                for r0 in range(0, CHUNK, b):
                    r = (r0 // (2 * b)) * 2 * b + b - 1
                    if (r0 // b) % 2:
                        pieces.append(q[r0:r0 + b] * jnp.exp2(g[r0:r0 + b] - g[r:r + 1]))
                    else:
                        pieces.append(k[r0:r0 + b] * jnp.exp2(g[r:r + 1] - g[r0:r0 + b]))
                x = jnp.concatenate(pieces, axis=0).astype(BF16)
            if b % SLAB == 0:
                upper = [j for j in range(n_slabs) if (j * SLAB // b) % 2]
                lhs = jnp.concatenate([x[j * SLAB:(j + 1) * SLAB] for j in upper], axis=0)
                ml = jnp.concatenate([m[j * SLAB:(j + 1) * SLAB] for j in upper], axis=0)
            else:
                upper, lhs, ml = list(range(n_slabs)), x, m
            for h, hc in enumerate(head_cols):
                s = ml * _dot_nt(lhs[:, hc], x[:, hc])
                for i, j in enumerate(upper):
                    add(h, j, s[i * SLAB:(i + 1) * SLAB])
            yield
        q_state = (q * jnp.exp2(g)).astype(BF16)
        k_end = k * jnp.exp2(g[CHUNK - 1:CHUNK] - g)
        chunk_decay = jnp.exp2(g[CHUNK - 1:CHUNK])
        o_intra, k_end_t, decay_col = [], [], []
        for h, hc in enumerate(head_cols):
            scores = jnp.concatenate(slabs[h], axis=0)
            scores = scores + diag[h] * masks_ref[DIAG_MASK] + pair[h] * masks_ref[PAIR_MASK]
            o_intra.append(_dot(scores.astype(BF16), vb[:, hc]))
            k_end_t.append(k_end[:, hc].T.astype(BF16))
            decay_col.append(jnp.broadcast_to(chunk_decay[:, hc], (SUBLANES, HEAD_DIM)).T[:, 0:1])
        yield
        return o_intra, q_state, k_end_t, decay_col, vb

    def inter(c, o_intra, q_state, k_end_t, decay_col, vb):
        rows = slice(c * CHUNK, (c + 1) * CHUNK)
        states = [st_ref[h] for h in range(HEADS)]
        if c == 0:
            states = [jnp.where(new_sequence, 0.0, s) for s in states]
        for h, hc in enumerate(head_cols):
            o_dst[rows, hc] = (o_intra[h] + _dot(q_state[:, hc], states[h].astype(BF16))).astype(BF16)
        for h, hc in enumerate(head_cols):
            st_ref[h] = decay_col[h] * states[h] + _dot(k_end_t[h], vb[:, hc])

    pending = yield from intra(0)
    for c in range(n_chunks):
        nxt = (yield from intra(c + 1)) if c + 1 < n_chunks else None
        inter(c, *pending)
        yield
        pending = nxt


HGRN_PIECES = 1 + (TILE // CHUNK) * (len(LEVEL_HALVES) + 3)


def _merge_pipe(x_ref, rows, o_src, src, onorm_ref, wa_ref, pw_ref, ps_ref, wb_ref, wo_ref, out_ref,
                pool_ref, halo_ref, new_sequence, pos0):
    col_pieces = [slice(lo, lo + MERGE_COLS) for lo in range(0, D_MODEL, MERGE_COLS)]
    o = o_src[...].astype(F32)
    heads = []
    for h in range(HEADS):
        oh = o[:, h * HEAD_DIM:(h + 1) * HEAD_DIM]
        heads.append(oh * lax.rsqrt(jnp.mean(oh * oh, axis=-1, keepdims=True) + EPS))
        if h % 2:
            yield
    on = (jnp.concatenate(heads, axis=-1) * onorm_ref[...] * src["og"][...]).astype(BF16)
    ya = []
    for c in col_pieces:
        ya.append(_dot(on, wa_ref[:, c]))
        yield

    u = src["u"][...].astype(F32)
    ext = jnp.concatenate([jnp.where(new_sequence, 0.0, halo_ref[...]), u], axis=0)
    halo_ref[...] = u[TILE - POOL_HALO:TILE, :]
    body = slice(POOL_HALO, 2 * POOL_HALO + TILE)
    pos = (pos0 + 1 + lax.broadcasted_iota(jnp.int32, (TILE, POOL_CH), 0)).astype(F32)
    groups = []
    for g, w in enumerate(POOL_WINDOWS):
        cols = slice(g * POOL_CH, (g + 1) * POOL_CH)
        s = ext[:, cols]
        d = 1
        while d < w:
            pool_ref[body, cols] = s
            s = s + pool_ref[POOL_HALO - d:2 * POOL_HALO + TILE - d, cols]
            d *= 2
        tok = u[:, cols]
        pooled = s[POOL_HALO:] / jnp.minimum(pos, float(w)) - tok
        groups.append(_dot(pooled.astype(BF16), pw_ref[g]))
        yield
    mixed = (jnp.concatenate(groups, axis=-1) * ps_ref[...]).astype(BF16)
    y = []
    for i, c in enumerate(col_pieces):
        yb = _dot(mixed, wb_ref[:, c])
        y.append((src["ga"][:, c] * ya[i] + src["gb"][:, c] * yb).astype(BF16))
        yield
    y = jnp.concatenate(y, axis=-1)
    for c in col_pieces:
        out_ref[rows, c] = x_ref[rows, c] + _dot(y, wo_ref[:, c])
        yield


MERGE_PIECES = HEADS // 2 + len(POOL_WINDOWS) + 3 * (D_MODEL // MERGE_COLS)


def _interleave(pipes):
    done = [0] * len(pipes)
    alive = [True] * len(pipes)
    while any(alive):
        i = min((i for i in range(len(pipes)) if alive[i]), key=lambda i: (done[i] + 1) / pipes[i][1])
        try:
            next(pipes[i][0])
            done[i] += 1
        except StopIteration:
            alive[i] = False


def _mixer_kernel(xp_ref, xm_ref, g_ref, w_ref, lb_ref, sums_ref, masks_ref, onorm_ref, wa_ref, pw_ref, ps_ref,
                  wb_ref, wo_ref, *rest, tiles_per_seq, n_cast):
    out_ref = rest[n_cast]
    _cast_slabs(rest[:n_cast], rest[n_cast + 1:2 * n_cast + 1])
    scratch = list(rest[2 * n_cast + 1:])

    def take(names):
        return {n: scratch.pop(0) for n in names}

    proj_buf = [take(PROJ_F32 + PROJ_BF16 + ("u",)) for _ in range(2)]
    merge_buf = [take(MERGE_NAMES) for _ in range(2)]
    o_buf = [scratch.pop(0) for _ in range(2)]
    st_ref, pool_ref, halo_ref = scratch
    i = pl.program_id(0)

    @pl.when(i == 0)
    def _():
        for ref in list(proj_buf[1].values()) + list(merge_buf[0].values()) + [o_buf[0], st_ref, halo_ref, pool_ref]:
            ref[...] = jnp.zeros_like(ref)

    for a in range(2):
        g = 2 * i + a
        rows = slice(a * TILE, (a + 1) * TILE)
        seq_h = lax.rem(g - 1 + tiles_per_seq, tiles_per_seq)
        seq_m = lax.rem(g - 2 + tiles_per_seq, tiles_per_seq)
        _interleave([
            (_proj_pipe(xp_ref[rows, :], g_ref, w_ref, lb_ref, proj_buf[a]), PROJ_PIECES),
            (_hgrn_pipe(proj_buf[1 - a], o_buf[1 - a], merge_buf[1 - a], sums_ref, masks_ref, st_ref,
                        seq_h == 0), HGRN_PIECES),
            (_merge_pipe(xm_ref, rows, o_buf[a], merge_buf[a], onorm_ref, wa_ref, pw_ref, ps_ref, wb_ref, wo_ref,
                         out_ref, pool_ref, halo_ref, seq_m == 0, seq_m * TILE), MERGE_PIECES),
        ])


def _mixer(x, g, w_in, hgrn_lb, onorm, wa, pw, ps, wb, wo, seq, cast=()):
    t = x.shape[0]
    n_steps = t // PAIR + 1
    cast_in, cast_out, cast_shapes = _cast_specs(cast, n_steps)
    sums, masks = _level_tables()
    sums2 = jnp.asarray(np.concatenate([sums, sums], axis=1), BF16)
    tile = lambda width, dtype: pltpu.VMEM((TILE, width), dtype)
    proj_slot = ([tile(HG_WIDTH, F32)] * len(PROJ_F32) + [tile(HG_WIDTH, BF16)] * len(PROJ_BF16)
                 + [tile(POOL_WIDTH, BF16)])
    merge_slot = [tile(HG_WIDTH, BF16)] * 3 + [tile(POOL_WIDTH, BF16)]
    return pl.pallas_call(
        functools.partial(_mixer_kernel, tiles_per_seq=seq // TILE, n_cast=len(cast)),
        grid=(n_steps,),
        in_specs=[pl.BlockSpec((PAIR, D_MODEL), lambda i: (jnp.minimum(i, n_steps - 2), 0)),
                  pl.BlockSpec((PAIR, D_MODEL), lambda i: (jnp.maximum(i - 1, 0), 0)),
                  _const_spec((1, D_MODEL)), _const_spec((D_MODEL, IN_SPLITS[-1])), _const_spec(hgrn_lb.shape),
                  _const_spec(sums2.shape), _const_spec(masks.shape),
                  _const_spec((1, HG_WIDTH)), _const_spec((HG_WIDTH, D_MODEL)),
                  _const_spec((len(POOL_WINDOWS), POOL_CH, POOL_CH)), _const_spec((1, POOL_WIDTH)),
                  _const_spec((POOL_WIDTH, D_MODEL)), _const_spec((D_MODEL, D_MODEL))] + cast_in,
        out_specs=[pl.BlockSpec((PAIR, D_MODEL), lambda i: (jnp.maximum(i - 1, 0), 0))] + cast_out,
        out_shape=[jax.ShapeDtypeStruct((t, D_MODEL), F32)] + cast_shapes,
        scratch_shapes=(proj_slot * 2 + merge_slot * 2 + [tile(HG_WIDTH, BF16)] * 2
                        + [pltpu.VMEM((HEADS, HEAD_DIM, HEAD_DIM), F32),
                           pltpu.VMEM((2 * POOL_HALO + TILE, POOL_WIDTH), F32),
                           pltpu.VMEM((POOL_HALO, POOL_WIDTH), F32)]),
        compiler_params=_params(1, sequential=True),
        name="mixer",
    )(x, x, g, w_in, hgrn_lb, sums2, jnp.asarray(masks), onorm, wa, pw, ps, wb, wo, *cast)


def _ffn_ple_kernel(x_ref, p_ref, g_ref, w1_ref, w3_ref, w2_ref, gn_ref, wg_ref, wp_ref, pn_ref, fn_ref, o_ref):
    x = x_ref[...]
    e = _rms(_dot(p_ref[...].astype(BF16), wp_ref[...]), pn_ref[...])
    h = _rms(x, g_ref[...]).astype(BF16)
    act = (_silu(_dot(h, w1_ref[...])) * _dot(h, w3_ref[...])).astype(BF16)
    half = x.shape[0] // 2
    halves = (slice(0, half), slice(half, 2 * half))
    xs = [x[r] + 0.5 * _dot(act[r], w2_ref[...]) for r in halves]
    for r, xh in zip(halves, xs):
        gate = jax.nn.sigmoid(_dot(_rms(xh, gn_ref[...]).astype(BF16), wg_ref[...]))
        o_ref[r, :] = _rms(xh + gate * e[r], fn_ref[...])


def _ffn_ple(x, p, g, w1, w3, w2, gn, wg, wp, pn, fn, tm=512):
    t = x.shape[0]
    row = pl.BlockSpec((tm, D_MODEL), lambda i: (i, 0))
    vec = _const_spec((1, D_MODEL))
    return pl.pallas_call(
        _ffn_ple_kernel,
        grid=(t // tm,),
        in_specs=[row, pl.BlockSpec((tm, PLE_DIM), lambda i: (i, 0)), vec, _const_spec((D_MODEL, D_FF)),
                  _const_spec((D_MODEL, D_FF)), _const_spec((D_FF, D_MODEL)), vec,
                  _const_spec((D_MODEL, D_MODEL)), _const_spec((PLE_DIM, D_MODEL)), vec, vec],
        out_specs=row,
        out_shape=jax.ShapeDtypeStruct((t, D_MODEL), F32),
        compiler_params=_params(1),
        name="ffn_ple",
    )(x, p, g, w1, w3, w2, gn, wg, wp, pn, fn)


def kernel(x, p, ffn1_norm, ffn1_w1, ffn1_w3, ffn1_w2, mix_norm, w_in, hgrn_lb, hgrn_onorm, w_branch_a, pool_w, pool_scale, w_branch_b, w_out, ffn2_norm, ffn2_w1, ffn2_w3, ffn2_w2, ple_norm, ple_w_gate, ple_w_proj, ple_post_norm, final_norm):
    batch, seq, d = x.shape
    assert d == D_MODEL and p.shape[0] == 1 and hgrn_lb.shape == (2, HG_WIDTH)
    assert seq % PAIR == 0
    t = batch * seq
    bf = lambda w: w.astype(BF16)
    vec = lambda g: g.reshape(1, -1)

    xt = x.reshape(t, d)
    xt, w_in_b, wa, wo, wb, pw = _ffn(
        xt, vec(ffn1_norm[0]), bf(ffn1_w1[0]), bf(ffn1_w3[0]), bf(ffn1_w2[0]),
        cast=(w_in[0], w_branch_a[0], w_out[0], w_branch_b[0], pool_w[0].reshape(POOL_WIDTH, POOL_CH)))
    xt, w1, w3, w2, wg, wp = _mixer(
        xt, vec(mix_norm[0]), w_in_b, hgrn_lb, vec(hgrn_onorm[0]), wa, pw.reshape(pool_w.shape[1:]),
        vec(pool_scale[0]), wb, wo, seq, cast=(ffn2_w1[0], ffn2_w3[0], ffn2_w2[0], ple_w_gate[0], ple_w_proj[0]))
    out = _ffn_ple(xt, p[0].reshape(t, PLE_DIM), vec(ffn2_norm[0]), w1, w3, w2,
                   vec(ple_norm[0]), wg, wp, vec(ple_post_norm[0]), vec(final_norm))
    return out.reshape(batch, seq, d)
```

```python
import functools

import numpy as np
import jax
import jax.numpy as jnp
from jax import lax
from jax.experimental import pallas as pl
from jax.experimental.pallas import tpu as pltpu

D_MODEL = 1024
D_FF = 2816
PLE_DIM = 256
HEADS = 8
HEAD_DIM = 128
HG_WIDTH = HEADS * HEAD_DIM
POOL_WINDOWS = (2, 4, 8, 16)
POOL_CH = 128
POOL_WIDTH = len(POOL_WINDOWS) * POOL_CH
POOL_HALO = 16
IN_SPLITS = (0, 1024, 2048, 3072, 4096, 4608, 5632, 6656)
EPS = 1e-6
LOG2E = 1.4426950408889634

CHUNK = 64
FINE_HALVES = (2, 4)
LEVEL_HALVES = FINE_HALVES + (8, 16, 32)
SLAB = 16
SUBLANES = 8
DIAG_MASK = len(LEVEL_HALVES)
PAIR_MASK = DIAG_MASK + 1

TILE = 256
PAIR = 2 * TILE
PROJ_COLS = 256
MERGE_COLS = 256

V7X_VMEM_BYTES = 64 * 1024 * 1024
VMEM_LIMIT = V7X_VMEM_BYTES - 8 * 1024 * 1024

BF16 = jnp.bfloat16
F32 = jnp.float32


def _rms(x, g):
    return x * lax.rsqrt(jnp.mean(x * x, axis=-1, keepdims=True) + EPS) * g


def _silu(x):
    return x * jax.nn.sigmoid(x)


def _dot(a, b):
    return jnp.dot(a, b, preferred_element_type=F32)


def _dot_nt(a, b):
    return lax.dot_general(a, b, (((1,), (1,)), ((), ())), preferred_element_type=F32)


def _const_spec(shape):
    zeros = (0,) * len(shape)
    return pl.BlockSpec(shape, lambda *_: zeros, pipeline_mode=pl.Buffered(1))


def _params(n_grid_dims, sequential=False):
    sem = ("arbitrary" if sequential else "parallel",) * n_grid_dims
    return pltpu.CompilerParams(dimension_semantics=sem, vmem_limit_bytes=VMEM_LIMIT)


def _cast_specs(weights, n_steps):
    in_specs, out_specs, out_shapes = [], [], []
    for w in weights:
        rows, cols = w.shape
        slab = next(s for s in range(SLAB, rows + 1, SLAB) if rows % s == 0 and s * n_steps >= rows)
        last = rows // slab - 1
        index = lambda i, last=last: (jnp.minimum(i, last), 0)
        in_specs.append(pl.BlockSpec((slab, cols), index))
        out_specs.append(pl.BlockSpec((slab, cols), index))
        out_shapes.append(jax.ShapeDtypeStruct((rows, cols), BF16))
    return in_specs, out_specs, out_shapes


def _cast_slabs(src_refs, dst_refs):
    for src, dst in zip(src_refs, dst_refs):
        dst[...] = src[...].astype(BF16)


def _ffn_kernel(x_ref, g_ref, w1_ref, w3_ref, w2_ref, *rest):
    n_cast = len(rest) // 2
    o_ref = rest[n_cast]
    _cast_slabs(rest[:n_cast], rest[n_cast + 1:])
    x = x_ref[...]
    h = _rms(x, g_ref[...]).astype(BF16)
    a = _dot(h, w1_ref[...])
    b = _dot(h, w3_ref[...])
    act = (_silu(a) * b).astype(BF16)
    o_ref[...] = x + 0.5 * _dot(act, w2_ref[...])


def _ffn(x, g, w1, w3, w2, cast=(), tm=512):
    t = x.shape[0]
    row = pl.BlockSpec((tm, D_MODEL), lambda i: (i, 0))
    cast_in, cast_out, cast_shapes = _cast_specs(cast, t // tm)
    return pl.pallas_call(
        _ffn_kernel,
        grid=(t // tm,),
        in_specs=[row, _const_spec((1, D_MODEL)), _const_spec((D_MODEL, D_FF)),
                  _const_spec((D_MODEL, D_FF)), _const_spec((D_FF, D_MODEL))] + cast_in,
        out_specs=[row] + cast_out,
        out_shape=[jax.ShapeDtypeStruct((t, D_MODEL), F32)] + cast_shapes,
        compiler_params=_params(1),
        name="ffn",
    )(x, g, w1, w3, w2, *cast)


def _level_tables():
    c = CHUNK
    sums = np.zeros((len(FINE_HALVES) + 1, c, c), np.float32)
    masks = np.zeros((len(LEVEL_HALVES) + 2, c, c), np.float32)
    masks[DIAG_MASK] = np.eye(c)
    masks[PAIR_MASK, 1::2] = np.eye(c)[0::2]
    for li, b in enumerate(LEVEL_HALVES):
        for t in range(c):
            base = (t // (2 * b)) * 2 * b
            r = base + b - 1
            if t > r:
                masks[li, t, base:r + 1] = 1.0
            if b in FINE_HALVES:
                sums[li, t, (r + 1 if t > r else t + 1):(t + 1 if t > r else r + 1)] = 1.0
    for t in range(c):
        sums[len(FINE_HALVES), t, :t + 1] = 1.0
    return sums.reshape(-1, c), masks


PROJ_F32 = ("q", "k")
PROJ_BF16 = ("v", "lfh", "lfl", "og", "ga", "gb")
MERGE_NAMES = ("og", "ga", "gb", "u")


def _proj_pipe(x, g_ref, w_ref, lb_ref, dst):
    h = _rms(x, g_ref[...]).astype(BF16)
    lbp = lb_ref[...]
    e = jnp.exp(lbp - jnp.max(lbp, axis=0, keepdims=True))
    lb = e[0:1, :] / jnp.sum(e, axis=0, keepdims=True)
    yield

    def pieces(i):
        for lo in range(IN_SPLITS[i], IN_SPLITS[i + 1], PROJ_COLS):
            yield slice(lo - IN_SPLITS[i], lo - IN_SPLITS[i] + PROJ_COLS), _dot(h, w_ref[:, lo:lo + PROJ_COLS])

    for c, z in pieces(0):
        dst["q"][:, c] = _silu(z)
        yield
    for c, z in pieces(1):
        f = lb[:, c] + (1.0 - lb[:, c]) * jax.nn.sigmoid(z)
        lf = jnp.log(f) * LOG2E
        hi = lf.astype(BF16)
        dst["lfh"][:, c] = hi
        dst["lfl"][:, c] = (lf - hi.astype(F32)).astype(BF16)
        dst["k"][:, c] = 1.0 - f
        yield
    for c, z in pieces(2):
        dst["v"][:, c] = z.astype(BF16)
        yield
    for c, z in pieces(3):
        dst["og"][:, c] = _silu(z).astype(BF16)
        yield
    for c, z in pieces(4):
        dst["u"][:, c] = z.astype(BF16)
        yield
    for name, i in (("ga", 5), ("gb", 6)):
        for c, z in pieces(i):
            dst[name][:, c] = jax.nn.sigmoid(z).astype(BF16)
            yield


PROJ_PIECES = 1 + IN_SPLITS[-1] // PROJ_COLS


def _hgrn_pipe(src, o_dst, merge_dst, sums_ref, masks_ref, st_ref, new_sequence):
    for name in MERGE_NAMES:
        merge_dst[name][...] = src[name][...]
    yield

    row = lax.broadcasted_iota(jnp.int32, (CHUNK, 1), 0)
    head_cols = [slice(h * HEAD_DIM, (h + 1) * HEAD_DIM) for h in range(HEADS)]
    n_slabs = CHUNK // SLAB
    n_chunks = TILE // CHUNK

    def head_sums(x):
        return [jnp.sum(x[:, hc], axis=-1, keepdims=True) for hc in head_cols]

    def prev_row(x):
        return jnp.concatenate([pltpu.roll(x[r:r + SUBLANES], 1, 0) for r in range(0, CHUNK, SUBLANES)], axis=0)

    def intra(c):
        rows = slice(c * CHUNK, (c + 1) * CHUNK)
        q = src["q"][rows, :]
        k = src["k"][rows, :]
        vb = src["v"][rows, :]
        expo = _dot(sums_ref[...], jnp.concatenate([src["lfh"][rows, :], src["lfl"][rows, :]], axis=0))
        g = expo[len(FINE_HALVES) * CHUNK:, :]

        diag = head_sums(q * k)
        pair = head_sums(q * (1.0 - k) * prev_row(k))
        yield

        slabs = [[None] * n_slabs for _ in range(HEADS)]

        def add(h, j, val):
            slabs[h][j] = val if slabs[h][j] is None else slabs[h][j] + val

        for li, b in enumerate(LEVEL_HALVES):
            m = masks_ref[li]
            if b in FINE_HALVES:
                qk = jnp.where((row % (2 * b)) >= b, q, k)
                x = (qk * jnp.exp2(expo[li * CHUNK:(li + 1) * CHUNK, :])).astype(BF16)
            else:
                pieces = []
                for r0 in range(0, CHUNK, b):
                    r = (r0 // (2 * b)) * 2 * b + b - 1
                    if (r0 // b) % 2:
                        pieces.append(q[r0:r0 + b] * jnp.exp2(g[r0:r0 + b] - g[r:r + 1]))
                    else:
                        pieces.append(k[r0:r0 + b] * jnp.exp2(g[r:r + 1] - g[r0:r0 + b]))
                x = jnp.concatenate(pieces, axis=0).astype(BF16)
            if b % SLAB == 0:
                upper = [j for j in range(n_slabs) if (j * SLAB // b) % 2]
                lhs = jnp.concatenate([x[j * SLAB:(j + 1) * SLAB] for j in upper], axis=0)
                ml = jnp.concatenate([m[j * SLAB:(j + 1) * SLAB] for j in upper], axis=0)
            else:
                upper, lhs, ml = list(range(n_slabs)), x, m
            keys = x
            if 2 * b == CHUNK:
                keys, ml = jnp.concatenate([x[:b], jnp.zeros_like(x[b:])], axis=0), None
            for h, hc in enumerate(head_cols):
                s = _dot_nt(lhs[:, hc], keys[:, hc])
                if ml is not None:
                    s = ml * s
                for i, j in enumerate(upper):
                    add(h, j, s[i * SLAB:(i + 1) * SLAB])
            yield
        q_state = (q * jnp.exp2(g)).astype(BF16)
        k_end = k * jnp.exp2(g[CHUNK - 1:CHUNK] - g)
        chunk_decay = jnp.exp2(g[CHUNK - 1:CHUNK])
        o_intra, k_end_t, decay_col = [], [], []
        for h, hc in enumerate(head_cols):
            scores = jnp.concatenate(slabs[h], axis=0)
            scores = scores + diag[h] * masks_ref[DIAG_MASK] + pair[h] * masks_ref[PAIR_MASK]
            o_intra.append(_dot(scores.astype(BF16), vb[:, hc]))
            k_end_t.append(k_end[:, hc].T.astype(BF16))
            decay_col.append(jnp.broadcast_to(chunk_decay[:, hc], (SUBLANES, HEAD_DIM)).T[:, 0:1])
        yield
        return o_intra, q_state, k_end_t, decay_col, vb

    def inter(c, o_intra, q_state, k_end_t, decay_col, vb):
        rows = slice(c * CHUNK, (c + 1) * CHUNK)
        states = [st_ref[h] for h in range(HEADS)]
        if c == 0:
            states = [jnp.where(new_sequence, 0.0, s) for s in states]
        for h, hc in enumerate(head_cols):
            o_dst[rows, hc] = (o_intra[h] + _dot(q_state[:, hc], states[h].astype(BF16))).astype(BF16)
        for h, hc in enumerate(head_cols):
            st_ref[h] = decay_col[h] * states[h] + _dot(k_end_t[h], vb[:, hc])

    pending = yield from intra(0)
    for c in range(n_chunks):
        nxt = (yield from intra(c + 1)) if c + 1 < n_chunks else None
        inter(c, *pending)
        yield
        pending = nxt


HGRN_PIECES = 1 + (TILE // CHUNK) * (len(LEVEL_HALVES) + 3)


def _merge_pipe(x_ref, rows, o_src, src, onorm_ref, wa_ref, pw_ref, ps_ref, wb_ref, wo_ref, out_ref,
                pool_ref, halo_ref, new_sequence, pos0):
    col_pieces = [slice(lo, lo + MERGE_COLS) for lo in range(0, D_MODEL, MERGE_COLS)]
    o = o_src[...].astype(F32)
    heads = []
    for h in range(HEADS):
        oh = o[:, h * HEAD_DIM:(h + 1) * HEAD_DIM]
        heads.append((oh * lax.rsqrt(jnp.mean(oh * oh, axis=-1, keepdims=True) + EPS)).astype(BF16))
        if h % 2:
            yield
    on = jnp.concatenate(heads, axis=-1) * onorm_ref[...].astype(BF16) * src["og"][...]
    ya = []
    for c in col_pieces:
        ya.append(_dot(on, wa_ref[:, c]))
        yield

    u = src["u"][...].astype(F32)
    ext = jnp.concatenate([jnp.where(new_sequence, 0.0, halo_ref[...]), u], axis=0)
    halo_ref[...] = u[TILE - POOL_HALO:TILE, :]
    body = slice(POOL_HALO, 2 * POOL_HALO + TILE)
    pos = (pos0 + 1 + lax.broadcasted_iota(jnp.int32, (TILE, POOL_CH), 0)).astype(F32)
    groups = []
    for g, w in enumerate(POOL_WINDOWS):
        cols = slice(g * POOL_CH, (g + 1) * POOL_CH)
        s = ext[:, cols]
        d = 1
        while d < w:
            pool_ref[body, cols] = s
            s = s + pool_ref[POOL_HALO - d:2 * POOL_HALO + TILE - d, cols]
            d *= 2
        tok = u[:, cols]
        pooled = s[POOL_HALO:] / jnp.minimum(pos, float(w)) - tok
        groups.append(_dot(pooled.astype(BF16), pw_ref[g]))
        yield
    mixed = (jnp.concatenate(groups, axis=-1) * ps_ref[...]).astype(BF16)
    y = []
    for i, c in enumerate(col_pieces):
        yb = _dot(mixed, wb_ref[:, c])
        y.append(src["ga"][:, c] * ya[i].astype(BF16) + src["gb"][:, c] * yb.astype(BF16))
        yield
    y = jnp.concatenate(y, axis=-1)
    for c in col_pieces:
        out_ref[rows, c] = x_ref[rows, c] + _dot(y, wo_ref[:, c])
        yield


MERGE_PIECES = HEADS // 2 + len(POOL_WINDOWS) + 3 * (D_MODEL // MERGE_COLS)


def _interleave(pipes):
    done = [0] * len(pipes)
    alive = [True] * len(pipes)
    while any(alive):
        i = min((i for i in range(len(pipes)) if alive[i]), key=lambda i: (done[i] + 1) / pipes[i][1])
        try:
            next(pipes[i][0])
            done[i] += 1
        except StopIteration:
            alive[i] = False


def _mixer_kernel(xp_ref, xm_ref, g_ref, w_ref, lb_ref, sums_ref, masks_ref, onorm_ref, wa_ref, pw_ref, ps_ref,
                  wb_ref, wo_ref, *rest, tiles_per_seq, n_cast):
    out_ref = rest[n_cast]
    _cast_slabs(rest[:n_cast], rest[n_cast + 1:2 * n_cast + 1])
    scratch = list(rest[2 * n_cast + 1:])

    def take(names):
        return {n: scratch.pop(0) for n in names}

    proj_buf = [take(PROJ_F32 + PROJ_BF16 + ("u",)) for _ in range(2)]
    merge_buf = [take(MERGE_NAMES) for _ in range(2)]
    o_buf = [scratch.pop(0) for _ in range(2)]
    st_ref, pool_ref, halo_ref = scratch
    i = pl.program_id(0)

    @pl.when(i == 0)
    def _():
        for ref in list(proj_buf[1].values()) + list(merge_buf[0].values()) + [o_buf[0], st_ref, halo_ref, pool_ref]:
            ref[...] = jnp.zeros_like(ref)

    for a in range(2):
        g = 2 * i + a
        rows = slice(a * TILE, (a + 1) * TILE)
        seq_h = lax.rem(g - 1 + tiles_per_seq, tiles_per_seq)
        seq_m = lax.rem(g - 2 + tiles_per_seq, tiles_per_seq)
        _interleave([
            (_proj_pipe(xp_ref[rows, :], g_ref, w_ref, lb_ref, proj_buf[a]), PROJ_PIECES),
            (_hgrn_pipe(proj_buf[1 - a], o_buf[1 - a], merge_buf[1 - a], sums_ref, masks_ref, st_ref,
                        seq_h == 0), HGRN_PIECES),
            (_merge_pipe(xm_ref, rows, o_buf[a], merge_buf[a], onorm_ref, wa_ref, pw_ref, ps_ref, wb_ref, wo_ref,
                         out_ref, pool_ref, halo_ref, seq_m == 0, seq_m * TILE), MERGE_PIECES),
        ])


def _mixer(x, g, w_in, hgrn_lb, onorm, wa, pw, ps, wb, wo, seq, cast=()):
    t = x.shape[0]
    n_steps = t // PAIR + 1
    cast_in, cast_out, cast_shapes = _cast_specs(cast, n_steps)
    sums, masks = _level_tables()
    sums2 = jnp.asarray(np.concatenate([sums, sums], axis=1), BF16)
    tile = lambda width, dtype: pltpu.VMEM((TILE, width), dtype)
    proj_slot = ([tile(HG_WIDTH, F32)] * len(PROJ_F32) + [tile(HG_WIDTH, BF16)] * len(PROJ_BF16)
                 + [tile(POOL_WIDTH, BF16)])
    merge_slot = [tile(HG_WIDTH, BF16)] * 3 + [tile(POOL_WIDTH, BF16)]
    return pl.pallas_call(
        functools.partial(_mixer_kernel, tiles_per_seq=seq // TILE, n_cast=len(cast)),
        grid=(n_steps,),
        in_specs=[pl.BlockSpec((PAIR, D_MODEL), lambda i: (jnp.minimum(i, n_steps - 2), 0)),
                  pl.BlockSpec((PAIR, D_MODEL), lambda i: (jnp.maximum(i - 1, 0), 0)),
                  _const_spec((1, D_MODEL)), _const_spec((D_MODEL, IN_SPLITS[-1])), _const_spec(hgrn_lb.shape),
                  _const_spec(sums2.shape), _const_spec(masks.shape),
                  _const_spec((1, HG_WIDTH)), _const_spec((HG_WIDTH, D_MODEL)),
                  _const_spec((len(POOL_WINDOWS), POOL_CH, POOL_CH)), _const_spec((1, POOL_WIDTH)),
                  _const_spec((POOL_WIDTH, D_MODEL)), _const_spec((D_MODEL, D_MODEL))] + cast_in,
        out_specs=[pl.BlockSpec((PAIR, D_MODEL), lambda i: (jnp.maximum(i - 1, 0), 0))] + cast_out,
        out_shape=[jax.ShapeDtypeStruct((t, D_MODEL), F32)] + cast_shapes,
        scratch_shapes=(proj_slot * 2 + merge_slot * 2 + [tile(HG_WIDTH, BF16)] * 2
                        + [pltpu.VMEM((HEADS, HEAD_DIM, HEAD_DIM), F32),
                           pltpu.VMEM((2 * POOL_HALO + TILE, POOL_WIDTH), F32),
                           pltpu.VMEM((POOL_HALO, POOL_WIDTH), F32)]),
        compiler_params=_params(1, sequential=True),
        name="mixer",
    )(x, x, g, w_in, hgrn_lb, sums2, jnp.asarray(masks), onorm, wa, pw, ps, wb, wo, *cast)


def _ffn_ple_kernel(x_ref, p_ref, g_ref, w1_ref, w3_ref, w2_ref, gn_ref, wg_ref, wp_ref, pn_ref, fn_ref, o_ref):
    x = x_ref[...]
    e = _rms(_dot(p_ref[...].astype(BF16), wp_ref[...]), pn_ref[...])
    h = _rms(x, g_ref[...]).astype(BF16)
    act = (_silu(_dot(h, w1_ref[...])) * _dot(h, w3_ref[...])).astype(BF16)
    half = x.shape[0] // 2
    halves = (slice(0, half), slice(half, 2 * half))
    xs = [x[r] + 0.5 * _dot(act[r], w2_ref[...]) for r in halves]
    for r, xh in zip(halves, xs):
        gate = jax.nn.sigmoid(_dot(_rms(xh, gn_ref[...]).astype(BF16), wg_ref[...]))
        o_ref[r, :] = _rms(xh + gate * e[r], fn_ref[...])


def _ffn_ple(x, p, g, w1, w3, w2, gn, wg, wp, pn, fn, tm=512):
    t = x.shape[0]
    row = pl.BlockSpec((tm, D_MODEL), lambda i: (i, 0))
    vec = _const_spec((1, D_MODEL))
    return pl.pallas_call(
        _ffn_ple_kernel,
        grid=(t // tm,),
        in_specs=[row, pl.BlockSpec((tm, PLE_DIM), lambda i: (i, 0)), vec, _const_spec((D_MODEL, D_FF)),
                  _const_spec((D_MODEL, D_FF)), _const_spec((D_FF, D_MODEL)), vec,
                  _const_spec((D_MODEL, D_MODEL)), _const_spec((PLE_DIM, D_MODEL)), vec, vec],
        out_specs=row,
        out_shape=jax.ShapeDtypeStruct((t, D_MODEL), F32),
        compiler_params=_params(1),
        name="ffn_ple",
    )(x, p, g, w1, w3, w2, gn, wg, wp, pn, fn)


def kernel(x, p, ffn1_norm, ffn1_w1, ffn1_w3, ffn1_w2, mix_norm, w_in, hgrn_lb, hgrn_onorm, w_branch_a, pool_w, pool_scale, w_branch_b, w_out, ffn2_norm, ffn2_w1, ffn2_w3, ffn2_w2, ple_norm, ple_w_gate, ple_w_proj, ple_post_norm, final_norm):
    batch, seq, d = x.shape
    assert d == D_MODEL and p.shape[0] == 1 and hgrn_lb.shape == (2, HG_WIDTH)
    assert seq % PAIR == 0
    t = batch * seq
    bf = lambda w: w.astype(BF16)
    vec = lambda g: g.reshape(1, -1)

    xt = x.reshape(t, d)
    xt, w_in_b, wa, wo, wb, pw = _ffn(
        xt, vec(ffn1_norm[0]), bf(ffn1_w1[0]), bf(ffn1_w3[0]), bf(ffn1_w2[0]),
        cast=(w_in[0], w_branch_a[0], w_out[0], w_branch_b[0], pool_w[0].reshape(POOL_WIDTH, POOL_CH)))
    xt, w1, w3, w2, wg, wp = _mixer(
        xt, vec(mix_norm[0]), w_in_b, hgrn_lb, vec(hgrn_onorm[0]), wa, pw.reshape(pool_w.shape[1:]),
        vec(pool_scale[0]), wb, wo, seq, cast=(ffn2_w1[0], ffn2_w3[0], ffn2_w2[0], ple_w_gate[0], ple_w_proj[0]))
    out = _ffn_ple(xt, p[0].reshape(t, PLE_DIM), vec(ffn2_norm[0]), w1, w3, w2,
                   vec(ple_norm[0]), wg, wp, vec(ple_post_norm[0]), vec(final_norm))
    return out.reshape(batch, seq, d)
```

```python
import functools

import numpy as np
import jax
import jax.numpy as jnp
from jax import lax
from jax.experimental import pallas as pl
from jax.experimental.pallas import tpu as pltpu

D_MODEL = 1024
D_FF = 2816
PLE_DIM = 256
HEADS = 8
HEAD_DIM = 128
HG_WIDTH = HEADS * HEAD_DIM
POOL_WINDOWS = (2, 4, 8, 16)
POOL_CH = 128
POOL_WIDTH = len(POOL_WINDOWS) * POOL_CH
POOL_HALO = 16
IN_SPLITS = (0, 1024, 2048, 3072, 4096, 4608, 5632, 6656)
EPS = 1e-6
LOG2E = 1.4426950408889634

CHUNK = 64
FINE_HALVES = (2, 4)
LEVEL_HALVES = FINE_HALVES + (8, 16, 32)
SLAB = 16
SUBLANES = 8
DIAG_MASK = len(LEVEL_HALVES)
PAIR_MASK = DIAG_MASK + 1

TILE = 256
PAIR = 2 * TILE
PROJ_COLS = 256
MERGE_COLS = 256

V7X_VMEM_BYTES = 64 * 1024 * 1024
VMEM_LIMIT = V7X_VMEM_BYTES - 8 * 1024 * 1024

BF16 = jnp.bfloat16
F32 = jnp.float32


def _rms(x, g):
    return x * lax.rsqrt(jnp.mean(x * x, axis=-1, keepdims=True) + EPS) * g


def _silu(x):
    return x * jax.nn.sigmoid(x)


def _dot(a, b):
    return jnp.dot(a, b, preferred_element_type=F32)


def _dot_nt(a, b):
    return lax.dot_general(a, b, (((1,), (1,)), ((), ())), preferred_element_type=F32)


def _const_spec(shape):
    zeros = (0,) * len(shape)
    return pl.BlockSpec(shape, lambda *_: zeros, pipeline_mode=pl.Buffered(1))


def _params(n_grid_dims, sequential=False):
    sem = ("arbitrary" if sequential else "parallel",) * n_grid_dims
    return pltpu.CompilerParams(dimension_semantics=sem, vmem_limit_bytes=VMEM_LIMIT)


def _cast_specs(weights, n_steps):
    in_specs, out_specs, out_shapes = [], [], []
    for w in weights:
        rows, cols = w.shape
        slab = next(s for s in range(SLAB, rows + 1, SLAB) if rows % s == 0 and s * n_steps >= rows)
        last = rows // slab - 1
        index = lambda i, last=last: (jnp.minimum(i, last), 0)
        in_specs.append(pl.BlockSpec((slab, cols), index))
        out_specs.append(pl.BlockSpec((slab, cols), index))
        out_shapes.append(jax.ShapeDtypeStruct((rows, cols), BF16))
    return in_specs, out_specs, out_shapes


def _cast_slabs(src_refs, dst_refs):
    for src, dst in zip(src_refs, dst_refs):
        dst[...] = src[...].astype(BF16)


def _ffn_kernel(x_ref, g_ref, w1_ref, w3_ref, w2_ref, *rest):
    n_cast = len(rest) // 2
    o_ref = rest[n_cast]
    _cast_slabs(rest[:n_cast], rest[n_cast + 1:])
    x = x_ref[...]
    h = _rms(x, g_ref[...]).astype(BF16)
    a = _dot(h, w1_ref[...])
    b = _dot(h, w3_ref[...])
    act = (_silu(a) * b).astype(BF16)
    o_ref[...] = x + 0.5 * _dot(act, w2_ref[...])


def _ffn(x, g, w1, w3, w2, cast=(), tm=512):
    t = x.shape[0]
    row = pl.BlockSpec((tm, D_MODEL), lambda i: (i, 0))
    cast_in, cast_out, cast_shapes = _cast_specs(cast, t // tm)
    return pl.pallas_call(
        _ffn_kernel,
        grid=(t // tm,),
        in_specs=[row, _const_spec((1, D_MODEL)), _const_spec((D_MODEL, D_FF)),
                  _const_spec((D_MODEL, D_FF)), _const_spec((D_FF, D_MODEL))] + cast_in,
        out_specs=[row] + cast_out,
        out_shape=[jax.ShapeDtypeStruct((t, D_MODEL), F32)] + cast_shapes,
        compiler_params=_params(1),
        name="ffn",
    )(x, g, w1, w3, w2, *cast)


def _level_tables():
    c = CHUNK
    sums = np.zeros((len(FINE_HALVES) + 1, c, c), np.float32)
    masks = np.zeros((len(LEVEL_HALVES) + 2, c, c), np.float32)
    masks[DIAG_MASK] = np.eye(c)
    masks[PAIR_MASK, 1::2] = np.eye(c)[0::2]
    for li, b in enumerate(LEVEL_HALVES):
        for t in range(c):
            base = (t // (2 * b)) * 2 * b
            r = base + b - 1
            if t > r:
                masks[li, t, base:r + 1] = 1.0
            if b in FINE_HALVES:
                sums[li, t, (r + 1 if t > r else t + 1):(t + 1 if t > r else r + 1)] = 1.0
    for t in range(c):
        sums[len(FINE_HALVES), t, :t + 1] = 1.0
    return sums.reshape(-1, c), masks


PROJ_F32 = ("q", "k")
PROJ_BF16 = ("v", "lfh", "lfl", "og", "ga", "gb")
MERGE_NAMES = ("og", "ga", "gb", "u")


def _proj_pipe(x, g_ref, w_ref, lb_ref, dst, consumed):
    h = _rms(x, g_ref[...]).astype(BF16)
    lbp = lb_ref[...]
    e = jnp.exp(lbp - jnp.max(lbp, axis=0, keepdims=True))
    lb = e[0:1, :] / jnp.sum(e, axis=0, keepdims=True)
    yield

    def pieces(i):
        for lo in range(IN_SPLITS[i], IN_SPLITS[i + 1], PROJ_COLS):
            yield slice(lo - IN_SPLITS[i], lo - IN_SPLITS[i] + PROJ_COLS), _dot(h, w_ref[:, lo:lo + PROJ_COLS])

    for c, z in pieces(0):
        dst["q"][:, c] = _silu(z)
        yield
    for c, z in pieces(1):
        f = lb[:, c] + (1.0 - lb[:, c]) * jax.nn.sigmoid(z)
        lf = jnp.log(f) * LOG2E
        hi = lf.astype(BF16)
        dst["lfh"][:, c] = hi
        dst["lfl"][:, c] = (lf - hi.astype(F32)).astype(BF16)
        dst["k"][:, c] = 1.0 - f
        yield
    for c, z in pieces(2):
        dst["v"][:, c] = z.astype(BF16)
        yield
    for c, z in pieces(3):
        assert ("og", c.start) in consumed
        dst["og"][:, c] = _silu(z).astype(BF16)
        yield
    for c, z in pieces(4):
        assert ("u", c.start) in consumed
        dst["u"][:, c] = z.astype(BF16)
        yield
    for name, i in (("ga", 5), ("gb", 6)):
        for c, z in pieces(i):
            assert (name, c.start) in consumed
            dst[name][:, c] = jax.nn.sigmoid(z).astype(BF16)
            yield


PROJ_PIECES = 1 + IN_SPLITS[-1] // PROJ_COLS


def _hgrn_pipe(src, o_dst, sums_ref, masks_ref, st_ref, new_sequence):
    row = lax.broadcasted_iota(jnp.int32, (CHUNK, 1), 0)
    head_cols = [slice(h * HEAD_DIM, (h + 1) * HEAD_DIM) for h in range(HEADS)]
    n_slabs = CHUNK // SLAB
    n_chunks = TILE // CHUNK

    def head_sums(x):
        return [jnp.sum(x[:, hc], axis=-1, keepdims=True) for hc in head_cols]

    def prev_row(x):
        return jnp.concatenate([pltpu.roll(x[r:r + SUBLANES], 1, 0) for r in range(0, CHUNK, SUBLANES)], axis=0)

    def intra(c):
        rows = slice(c * CHUNK, (c + 1) * CHUNK)
        q = src["q"][rows, :]
        k = src["k"][rows, :]
        vb = src["v"][rows, :]
        expo = _dot(sums_ref[...], jnp.concatenate([src["lfh"][rows, :], src["lfl"][rows, :]], axis=0))
        g = expo[len(FINE_HALVES) * CHUNK:, :]

        diag = head_sums(q * k)
        pair = head_sums(q * (1.0 - k) * prev_row(k))
        yield

        slabs = [[None] * n_slabs for _ in range(HEADS)]

        def add(h, j, val):
            slabs[h][j] = val if slabs[h][j] is None else slabs[h][j] + val

        for li, b in enumerate(LEVEL_HALVES):
            m = masks_ref[li]
            if b in FINE_HALVES:
                qk = jnp.where((row % (2 * b)) >= b, q, k)
                x = (qk * jnp.exp2(expo[li * CHUNK:(li + 1) * CHUNK, :])).astype(BF16)
            else:
                pieces = []
                for r0 in range(0, CHUNK, b):
                    r = (r0 // (2 * b)) * 2 * b + b - 1
                    if (r0 // b) % 2:
                        pieces.append(q[r0:r0 + b] * jnp.exp2(g[r0:r0 + b] - g[r:r + 1]))
                    else:
                        pieces.append(k[r0:r0 + b] * jnp.exp2(g[r:r + 1] - g[r0:r0 + b]))
                x = jnp.concatenate(pieces, axis=0).astype(BF16)
            if b % SLAB == 0:
                upper = [j for j in range(n_slabs) if (j * SLAB // b) % 2]
                lhs = jnp.concatenate([x[j * SLAB:(j + 1) * SLAB] for j in upper], axis=0)
                ml = jnp.concatenate([m[j * SLAB:(j + 1) * SLAB] for j in upper], axis=0)
            else:
                upper, lhs, ml = list(range(n_slabs)), x, m
            for h, hc in enumerate(head_cols):
                s = ml * _dot_nt(lhs[:, hc], x[:, hc])
                for i, j in enumerate(upper):
                    add(h, j, s[i * SLAB:(i + 1) * SLAB])
            yield
        q_state = (q * jnp.exp2(g)).astype(BF16)
        k_end = k * jnp.exp2(g[CHUNK - 1:CHUNK] - g)
        chunk_decay = jnp.exp2(g[CHUNK - 1:CHUNK])
        o_intra, k_end_t, decay_col = [], [], []
        for h, hc in enumerate(head_cols):
            scores = jnp.concatenate(slabs[h], axis=0)
            scores = scores + diag[h] * masks_ref[DIAG_MASK] + pair[h] * masks_ref[PAIR_MASK]
            o_intra.append(_dot(scores.astype(BF16), vb[:, hc]))
            k_end_t.append(k_end[:, hc].T.astype(BF16))
            decay_col.append(jnp.broadcast_to(chunk_decay[:, hc], (SUBLANES, HEAD_DIM)).T[:, 0:1])
        yield
        return o_intra, q_state, k_end_t, decay_col, vb

    def inter(c, o_intra, q_state, k_end_t, decay_col, vb):
        rows = slice(c * CHUNK, (c + 1) * CHUNK)
        states = [st_ref[h] for h in range(HEADS)]
        if c == 0:
            states = [jnp.where(new_sequence, 0.0, s) for s in states]
        for h, hc in enumerate(head_cols):
            o_dst[rows, hc] = (o_intra[h] + _dot(q_state[:, hc], states[h].astype(BF16))).astype(BF16)
        for h, hc in enumerate(head_cols):
            st_ref[h] = decay_col[h] * states[h] + _dot(k_end_t[h], vb[:, hc])

    pending = yield from intra(0)
    for c in range(n_chunks):
        nxt = (yield from intra(c + 1)) if c + 1 < n_chunks else None
        inter(c, *pending)
        yield
        pending = nxt


HGRN_PIECES = (TILE // CHUNK) * (len(LEVEL_HALVES) + 3)


def _merge_pipe(x_ref, rows, o_src, src, onorm_ref, wa_ref, pw_ref, ps_ref, wb_ref, wo_ref, out_ref,
                pool_ref, halo_ref, new_sequence, pos0, consumed):
    col_pieces = [slice(lo, lo + MERGE_COLS) for lo in range(0, D_MODEL, MERGE_COLS)]

    def read(name, c=slice(None)):
        width = src[name].shape[1]
        lo, hi = c.indices(width)[:2]
        consumed.update((name, s) for s in range(lo, hi, PROJ_COLS))
        return src[name][:, c]

    o = o_src[...].astype(F32)
    heads = []
    for h in range(HEADS):
        oh = o[:, h * HEAD_DIM:(h + 1) * HEAD_DIM]
        heads.append(oh * lax.rsqrt(jnp.mean(oh * oh, axis=-1, keepdims=True) + EPS))
        if h % 2:
            yield
    on = (jnp.concatenate(heads, axis=-1) * onorm_ref[...] * read("og")).astype(BF16)
    ya = []
    for c in col_pieces:
        ya.append(_dot(on, wa_ref[:, c]))
        yield

    u = read("u").astype(F32)
    ext = jnp.concatenate([jnp.where(new_sequence, 0.0, halo_ref[...]), u], axis=0)
    halo_ref[...] = u[TILE - POOL_HALO:TILE, :]
    body = slice(POOL_HALO, 2 * POOL_HALO + TILE)
    pos = (pos0 + 1 + lax.broadcasted_iota(jnp.int32, (TILE, POOL_CH), 0)).astype(F32)
    groups = []
    for g, w in enumerate(POOL_WINDOWS):
        cols = slice(g * POOL_CH, (g + 1) * POOL_CH)
        s = ext[:, cols]
        d = 1
        while d < w:
            pool_ref[body, cols] = s
            s = s + pool_ref[POOL_HALO - d:2 * POOL_HALO + TILE - d, cols]
            d *= 2
        tok = u[:, cols]
        pooled = s[POOL_HALO:] / jnp.minimum(pos, float(w)) - tok
        groups.append(_dot(pooled.astype(BF16), pw_ref[g]))
        yield
    mixed = (jnp.concatenate(groups, axis=-1) * ps_ref[...]).astype(BF16)
    y = []
    for i, c in enumerate(col_pieces):
        yb = _dot(mixed, wb_ref[:, c])
        y.append((read("ga", c) * ya[i] + read("gb", c) * yb).astype(BF16))
        yield
    y = jnp.concatenate(y, axis=-1)
    for c in col_pieces:
        out_ref[rows, c] = x_ref[rows, c] + _dot(y, wo_ref[:, c])
        yield


MERGE_PIECES = HEADS // 2 + len(POOL_WINDOWS) + 3 * (D_MODEL // MERGE_COLS)


def _interleave(pipes):
    done = [0] * len(pipes)
    alive = [True] * len(pipes)
    while any(alive):
        i = min((i for i in range(len(pipes)) if alive[i]), key=lambda i: (done[i] + 1) / pipes[i][1])
        try:
            next(pipes[i][0])
            done[i] += 1
        except StopIteration:
            alive[i] = False


def _mixer_kernel(xp_ref, xm_ref, g_ref, w_ref, lb_ref, sums_ref, masks_ref, onorm_ref, wa_ref, pw_ref, ps_ref,
                  wb_ref, wo_ref, *rest, tiles_per_seq, n_cast):
    out_ref = rest[n_cast]
    _cast_slabs(rest[:n_cast], rest[n_cast + 1:2 * n_cast + 1])
    scratch = list(rest[2 * n_cast + 1:])

    def take(names):
        return {n: scratch.pop(0) for n in names}

    proj_buf = [take(PROJ_F32 + PROJ_BF16 + ("u",)) for _ in range(2)]
    o_buf = [scratch.pop(0) for _ in range(2)]
    st_ref, pool_ref, halo_ref = scratch
    i = pl.program_id(0)

    @pl.when(i == 0)
    def _():
        early = list(proj_buf[1].values()) + [proj_buf[0][name] for name in MERGE_NAMES]
        for ref in early + [o_buf[0], st_ref, halo_ref, pool_ref]:
            ref[...] = jnp.zeros_like(ref)

    for a in range(2):
        g = 2 * i + a
        rows = slice(a * TILE, (a + 1) * TILE)
        seq_h = lax.rem(g - 1 + tiles_per_seq, tiles_per_seq)
        seq_m = lax.rem(g - 2 + tiles_per_seq, tiles_per_seq)
        consumed = set()
        _interleave([
            (_proj_pipe(xp_ref[rows, :], g_ref, w_ref, lb_ref, proj_buf[a], consumed), PROJ_PIECES),
            (_hgrn_pipe(proj_buf[1 - a], o_buf[1 - a], sums_ref, masks_ref, st_ref, seq_h == 0), HGRN_PIECES),
            (_merge_pipe(xm_ref, rows, o_buf[a], proj_buf[a], onorm_ref, wa_ref, pw_ref, ps_ref, wb_ref, wo_ref,
                         out_ref, pool_ref, halo_ref, seq_m == 0, seq_m * TILE, consumed), MERGE_PIECES),
        ])


def _mixer(x, g, w_in, hgrn_lb, onorm, wa, pw, ps, wb, wo, seq, cast=()):
    t = x.shape[0]
    n_steps = t // PAIR + 1
    cast_in, cast_out, cast_shapes = _cast_specs(cast, n_steps)
    sums, masks = _level_tables()
    sums2 = jnp.asarray(np.concatenate([sums, sums], axis=1), BF16)
    tile = lambda width, dtype: pltpu.VMEM((TILE, width), dtype)
    proj_slot = ([tile(HG_WIDTH, F32)] * len(PROJ_F32) + [tile(HG_WIDTH, BF16)] * len(PROJ_BF16)
                 + [tile(POOL_WIDTH, BF16)])
    return pl.pallas_call(
        functools.partial(_mixer_kernel, tiles_per_seq=seq // TILE, n_cast=len(cast)),
        grid=(n_steps,),
        in_specs=[pl.BlockSpec((PAIR, D_MODEL), lambda i: (jnp.minimum(i, n_steps - 2), 0)),
                  pl.BlockSpec((PAIR, D_MODEL), lambda i: (jnp.maximum(i - 1, 0), 0)),
                  _const_spec((1, D_MODEL)), _const_spec((D_MODEL, IN_SPLITS[-1])), _const_spec(hgrn_lb.shape),
                  _const_spec(sums2.shape), _const_spec(masks.shape),
                  _const_spec((1, HG_WIDTH)), _const_spec((HG_WIDTH, D_MODEL)),
                  _const_spec((len(POOL_WINDOWS), POOL_CH, POOL_CH)), _const_spec((1, POOL_WIDTH)),
                  _const_spec((POOL_WIDTH, D_MODEL)), _const_spec((D_MODEL, D_MODEL))] + cast_in,
        out_specs=[pl.BlockSpec((PAIR, D_MODEL), lambda i: (jnp.maximum(i - 1, 0), 0))] + cast_out,
        out_shape=[jax.ShapeDtypeStruct((t, D_MODEL), F32)] + cast_shapes,
        scratch_shapes=(proj_slot * 2 + [tile(HG_WIDTH, BF16)] * 2
                        + [pltpu.VMEM((HEADS, HEAD_DIM, HEAD_DIM), F32),
                           pltpu.VMEM((2 * POOL_HALO + TILE, POOL_WIDTH), F32),
                           pltpu.VMEM((POOL_HALO, POOL_WIDTH), F32)]),
        compiler_params=_params(1, sequential=True),
        name="mixer",
    )(x, x, g, w_in, hgrn_lb, sums2, jnp.asarray(masks), onorm, wa, pw, ps, wb, wo, *cast)


def _ffn_ple_kernel(x_ref, p_ref, g_ref, w1_ref, w3_ref, w2_ref, gn_ref, wg_ref, wp_ref, pn_ref, fn_ref, o_ref):
    x = x_ref[...]
    e = _rms(_dot(p_ref[...].astype(BF16), wp_ref[...]), pn_ref[...])
    h = _rms(x, g_ref[...]).astype(BF16)
    act = (_silu(_dot(h, w1_ref[...])) * _dot(h, w3_ref[...])).astype(BF16)
    half = x.shape[0] // 2
    halves = (slice(0, half), slice(half, 2 * half))
    xs = [x[r] + 0.5 * _dot(act[r], w2_ref[...]) for r in halves]
    for r, xh in zip(halves, xs):
        gate = jax.nn.sigmoid(_dot(_rms(xh, gn_ref[...]).astype(BF16), wg_ref[...]))
        o_ref[r, :] = _rms(xh + gate * e[r], fn_ref[...])


def _ffn_ple(x, p, g, w1, w3, w2, gn, wg, wp, pn, fn, tm=512):
    t = x.shape[0]
    row = pl.BlockSpec((tm, D_MODEL), lambda i: (i, 0))
    vec = _const_spec((1, D_MODEL))
    return pl.pallas_call(
        _ffn_ple_kernel,
        grid=(t // tm,),
        in_specs=[row, pl.BlockSpec((tm, PLE_DIM), lambda i: (i, 0)), vec, _const_spec((D_MODEL, D_FF)),
                  _const_spec((D_MODEL, D_FF)), _const_spec((D_FF, D_MODEL)), vec,
                  _const_spec((D_MODEL, D_MODEL)), _const_spec((PLE_DIM, D_MODEL)), vec, vec],
        out_specs=row,
        out_shape=jax.ShapeDtypeStruct((t, D_MODEL), F32),
        compiler_params=_params(1),
        name="ffn_ple",
    )(x, p, g, w1, w3, w2, gn, wg, wp, pn, fn)


def kernel(x, p, ffn1_norm, ffn1_w1, ffn1_w3, ffn1_w2, mix_norm, w_in, hgrn_lb, hgrn_onorm, w_branch_a, pool_w, pool_scale, w_branch_b, w_out, ffn2_norm, ffn2_w1, ffn2_w3, ffn2_w2, ple_norm, ple_w_gate, ple_w_proj, ple_post_norm, final_norm):
    batch, seq, d = x.shape
    assert d == D_MODEL and p.shape[0] == 1 and hgrn_lb.shape == (2, HG_WIDTH)
    assert seq % PAIR == 0
    t = batch * seq
    bf = lambda w: w.astype(BF16)
    vec = lambda g: g.reshape(1, -1)

    xt = x.reshape(t, d)
    xt, w_in_b, wa, wo, wb, pw = _ffn(
        xt, vec(ffn1_norm[0]), bf(ffn1_w1[0]), bf(ffn1_w3[0]), bf(ffn1_w2[0]),
        cast=(w_in[0], w_branch_a[0], w_out[0], w_branch_b[0], pool_w[0].reshape(POOL_WIDTH, POOL_CH)))
    xt, w1, w3, w2, wg, wp = _mixer(
        xt, vec(mix_norm[0]), w_in_b, hgrn_lb, vec(hgrn_onorm[0]), wa, pw.reshape(pool_w.shape[1:]),
        vec(pool_scale[0]), wb, wo, seq, cast=(ffn2_w1[0], ffn2_w3[0], ffn2_w2[0], ple_w_gate[0], ple_w_proj[0]))
    out = _ffn_ple(xt, p[0].reshape(t, PLE_DIM), vec(ffn2_norm[0]), w1, w3, w2,
                   vec(ple_norm[0]), wg, wp, vec(ple_post_norm[0]), vec(final_norm))
    return out.reshape(batch, seq, d)
```

```python
import functools

import numpy as np
import jax
import jax.numpy as jnp
from jax import lax
from jax.experimental import pallas as pl
from jax.experimental.pallas import tpu as pltpu

D_MODEL = 1024
D_FF = 2816
PLE_DIM = 256
HEADS = 8
HEAD_DIM = 128
HG_WIDTH = HEADS * HEAD_DIM
POOL_WINDOWS = (2, 4, 8, 16)
POOL_CH = 128
POOL_WIDTH = len(POOL_WINDOWS) * POOL_CH
POOL_HALO = 16
IN_SPLITS = (0, 1024, 2048, 3072, 4096, 4608, 5632, 6656)
EPS = 1e-6

CHUNK = 64
FINE_HALVES = (2, 4)
LEVEL_HALVES = FINE_HALVES + (8, 16, 32)
SLAB = 16
SUBLANES = 8
DIAG_MASK = len(LEVEL_HALVES)
PAIR_MASK = DIAG_MASK + 1

TILE = 256
PAIR = 2 * TILE
PROJ_COLS = 256
MERGE_COLS = 256

V7X_VMEM_BYTES = 64 * 1024 * 1024
VMEM_LIMIT = V7X_VMEM_BYTES - 8 * 1024 * 1024

BF16 = jnp.bfloat16
F32 = jnp.float32


def _rms(x, g):
    return x * lax.rsqrt(jnp.mean(x * x, axis=-1, keepdims=True) + EPS) * g


def _sigmoid(x):
    return 0.5 * jnp.tanh(0.5 * x) + 0.5


def _silu(x):
    h = 0.5 * x
    return h * jnp.tanh(h) + h


def _dot(a, b):
    return jnp.dot(a, b, preferred_element_type=F32)


def _dot_nt(a, b):
    return lax.dot_general(a, b, (((1,), (1,)), ((), ())), preferred_element_type=F32)


def _const_spec(shape):
    zeros = (0,) * len(shape)
    return pl.BlockSpec(shape, lambda *_: zeros, pipeline_mode=pl.Buffered(1))


def _params(n_grid_dims, sequential=False):
    sem = ("arbitrary" if sequential else "parallel",) * n_grid_dims
    return pltpu.CompilerParams(dimension_semantics=sem, vmem_limit_bytes=VMEM_LIMIT)


def _cast_specs(weights, n_steps):
    in_specs, out_specs, out_shapes = [], [], []
    for w in weights:
        rows, cols = w.shape
        slab = next(s for s in range(SLAB, rows + 1, SLAB) if rows % s == 0 and s * n_steps >= rows)
        last = rows // slab - 1
        index = lambda i, last=last: (jnp.minimum(i, last), 0)
        in_specs.append(pl.BlockSpec((slab, cols), index))
        out_specs.append(pl.BlockSpec((slab, cols), index))
        out_shapes.append(jax.ShapeDtypeStruct((rows, cols), BF16))
    return in_specs, out_specs, out_shapes


def _cast_slabs(src_refs, dst_refs):
    for src, dst in zip(src_refs, dst_refs):
        dst[...] = src[...].astype(BF16)


def _ffn_kernel(x_ref, g_ref, w1_ref, w3_ref, w2_ref, *rest):
    n_cast = len(rest) // 2
    o_ref = rest[n_cast]
    _cast_slabs(rest[:n_cast], rest[n_cast + 1:])
    x = x_ref[...]
    h = _rms(x, g_ref[...]).astype(BF16)
    a = _dot(h, w1_ref[...])
    b = _dot(h, w3_ref[...])
    act = (_silu(a) * b).astype(BF16)
    o_ref[...] = x + 0.5 * _dot(act, w2_ref[...])


def _ffn(x, g, w1, w3, w2, cast=(), tm=512):
    t = x.shape[0]
    row = pl.BlockSpec((tm, D_MODEL), lambda i: (i, 0))
    cast_in, cast_out, cast_shapes = _cast_specs(cast, t // tm)
    return pl.pallas_call(
        _ffn_kernel,
        grid=(t // tm,),
        in_specs=[row, _const_spec((1, D_MODEL)), _const_spec((D_MODEL, D_FF)),
                  _const_spec((D_MODEL, D_FF)), _const_spec((D_FF, D_MODEL))] + cast_in,
        out_specs=[row] + cast_out,
        out_shape=[jax.ShapeDtypeStruct((t, D_MODEL), F32)] + cast_shapes,
        compiler_params=_params(1),
        name="ffn",
    )(x, g, w1, w3, w2, *cast)


def _level_tables():
    c = CHUNK
    sums = np.zeros((len(FINE_HALVES) + 1, c, c), np.float32)
    masks = np.zeros((len(LEVEL_HALVES) + 2, c, c), np.float32)
    masks[DIAG_MASK] = np.eye(c)
    masks[PAIR_MASK, 1::2] = np.eye(c)[0::2]
    for li, b in enumerate(LEVEL_HALVES):
        for t in range(c):
            base = (t // (2 * b)) * 2 * b
            r = base + b - 1
            if t > r:
                masks[li, t, base:r + 1] = 1.0
            if b in FINE_HALVES:
                sums[li, t, (r + 1 if t > r else t + 1):(t + 1 if t > r else r + 1)] = 1.0
    for t in range(c):
        sums[len(FINE_HALVES), t, :t + 1] = 1.0
    return sums.reshape(-1, c), masks


PROJ_F32 = ("q", "k")
PROJ_BF16 = ("v", "lfh", "lfl", "og", "ga", "gb")
MERGE_NAMES = ("og", "ga", "gb", "u")


def _proj_pipe(x, g_ref, w_ref, lb_ref, dst):
    h = _rms(x, g_ref[...]).astype(BF16)
    lbp = lb_ref[...]
    e = jnp.exp(lbp - jnp.max(lbp, axis=0, keepdims=True))
    lb = e[0:1, :] / jnp.sum(e, axis=0, keepdims=True)
    yield

    def pieces(i):
        for lo in range(IN_SPLITS[i], IN_SPLITS[i + 1], PROJ_COLS):
            yield slice(lo - IN_SPLITS[i], lo - IN_SPLITS[i] + PROJ_COLS), _dot(h, w_ref[:, lo:lo + PROJ_COLS])

    for c, z in pieces(0):
        dst["q"][:, c] = _silu(z)
        yield
    for c, z in pieces(1):
        f = lb[:, c] + (1.0 - lb[:, c]) * _sigmoid(z)
        lf = jnp.log2(f)
        hi = lf.astype(BF16)
        dst["lfh"][:, c] = hi
        dst["lfl"][:, c] = (lf - hi.astype(F32)).astype(BF16)
        dst["k"][:, c] = 1.0 - f
        yield
    for c, z in pieces(2):
        dst["v"][:, c] = z.astype(BF16)
        yield
    for c, z in pieces(3):
        dst["og"][:, c] = _silu(z).astype(BF16)
        yield
    for c, z in pieces(4):
        dst["u"][:, c] = z.astype(BF16)
        yield
    for name, i in (("ga", 5), ("gb", 6)):
        for c, z in pieces(i):
            dst[name][:, c] = _sigmoid(z).astype(BF16)
            yield


PROJ_PIECES = 1 + IN_SPLITS[-1] // PROJ_COLS


def _hgrn_pipe(src, o_dst, merge_dst, sums_ref, masks_ref, st_ref, new_sequence):
    for name in MERGE_NAMES:
        merge_dst[name][...] = src[name][...]
    yield

    row = lax.broadcasted_iota(jnp.int32, (CHUNK, 1), 0)
    head_cols = [slice(h * HEAD_DIM, (h + 1) * HEAD_DIM) for h in range(HEADS)]
    n_slabs = CHUNK // SLAB
    n_chunks = TILE // CHUNK

    def head_sums(x):
        return [jnp.sum(x[:, hc], axis=-1, keepdims=True) for hc in head_cols]

    def prev_row(x):
        return jnp.concatenate([pltpu.roll(x[r:r + SUBLANES], 1, 0) for r in range(0, CHUNK, SUBLANES)], axis=0)

    def intra(c):
        rows = slice(c * CHUNK, (c + 1) * CHUNK)
        q = src["q"][rows, :]
        k = src["k"][rows, :]
        vb = src["v"][rows, :]
        expo = _dot(sums_ref[...], jnp.concatenate([src["lfh"][rows, :], src["lfl"][rows, :]], axis=0))
        g = expo[len(FINE_HALVES) * CHUNK:, :]

        diag = head_sums(q * k)
        pair = head_sums(q * (1.0 - k) * prev_row(k))
        yield

        slabs = [[None] * n_slabs for _ in range(HEADS)]

        def add(h, j, val):
            slabs[h][j] = val if slabs[h][j] is None else slabs[h][j] + val

        for li, b in enumerate(LEVEL_HALVES):
            m = masks_ref[li]
            if b in FINE_HALVES:
                qk = jnp.where((row % (2 * b)) >= b, q, k)
                x = (qk * jnp.exp2(expo[li * CHUNK:(li + 1) * CHUNK, :])).astype(BF16)
            else:
                pieces = []
                for r0 in range(0, CHUNK, b):
                    r = (r0 // (2 * b)) * 2 * b + b - 1
                    if (r0 // b) % 2:
                        pieces.append(q[r0:r0 + b] * jnp.exp2(g[r0:r0 + b] - g[r:r + 1]))
                    else:
                        pieces.append(k[r0:r0 + b] * jnp.exp2(g[r:r + 1] - g[r0:r0 + b]))
                x = jnp.concatenate(pieces, axis=0).astype(BF16)
            if b % SLAB == 0:
                upper = [j for j in range(n_slabs) if (j * SLAB // b) % 2]
                lhs = jnp.concatenate([x[j * SLAB:(j + 1) * SLAB] for j in upper], axis=0)
                ml = jnp.concatenate([m[j * SLAB:(j + 1) * SLAB] for j in upper], axis=0)
            else:
                upper, lhs, ml = list(range(n_slabs)), x, m
            for h, hc in enumerate(head_cols):
                s = ml * _dot_nt(lhs[:, hc], x[:, hc])
                for i, j in enumerate(upper):
                    add(h, j, s[i * SLAB:(i + 1) * SLAB])
            yield
        q_state = (q * jnp.exp2(g)).astype(BF16)
        k_end = k * jnp.exp2(g[CHUNK - 1:CHUNK] - g)
        chunk_decay = jnp.exp2(g[CHUNK - 1:CHUNK])
        o_intra, k_end_t, decay_col = [], [], []
        for h, hc in enumerate(head_cols):
            scores = jnp.concatenate(slabs[h], axis=0)
            scores = scores + diag[h] * masks_ref[DIAG_MASK] + pair[h] * masks_ref[PAIR_MASK]
            o_intra.append(_dot(scores.astype(BF16), vb[:, hc]))
            k_end_t.append(k_end[:, hc].T.astype(BF16))
            decay_col.append(jnp.broadcast_to(chunk_decay[:, hc], (SUBLANES, HEAD_DIM)).T[:, 0:1])
        yield
        return o_intra, q_state, k_end_t, decay_col, vb

    def inter(c, o_intra, q_state, k_end_t, decay_col, vb):
        rows = slice(c * CHUNK, (c + 1) * CHUNK)
        states = [st_ref[h] for h in range(HEADS)]
        if c == 0:
            states = [jnp.where(new_sequence, 0.0, s) for s in states]
        for h, hc in enumerate(head_cols):
            o_dst[rows, hc] = (o_intra[h] + _dot(q_state[:, hc], states[h].astype(BF16))).astype(BF16)
        for h, hc in enumerate(head_cols):
            st_ref[h] = decay_col[h] * states[h] + _dot(k_end_t[h], vb[:, hc])

    pending = yield from intra(0)
    for c in range(n_chunks):
        nxt = (yield from intra(c + 1)) if c + 1 < n_chunks else None
        inter(c, *pending)
        yield
        pending = nxt


HGRN_PIECES = 1 + (TILE // CHUNK) * (len(LEVEL_HALVES) + 3)


def _merge_pipe(x_ref, rows, o_src, src, onorm_ref, wa_ref, pw_ref, ps_ref, wb_ref, wo_ref, out_ref,
                pool_ref, halo_ref, new_sequence, pos0):
    col_pieces = [slice(lo, lo + MERGE_COLS) for lo in range(0, D_MODEL, MERGE_COLS)]
    o = o_src[...].astype(F32)
    heads = []
    for h in range(HEADS):
        oh = o[:, h * HEAD_DIM:(h + 1) * HEAD_DIM]
        heads.append(oh * lax.rsqrt(jnp.mean(oh * oh, axis=-1, keepdims=True) + EPS))
        if h % 2:
            yield
    on = (jnp.concatenate(heads, axis=-1) * onorm_ref[...] * src["og"][...]).astype(BF16)
    ya = []
    for c in col_pieces:
        ya.append(_dot(on, wa_ref[:, c]))
        yield

    u = src["u"][...].astype(F32)
    ext = jnp.concatenate([jnp.where(new_sequence, 0.0, halo_ref[...]), u], axis=0)
    halo_ref[...] = u[TILE - POOL_HALO:TILE, :]
    body = slice(POOL_HALO, 2 * POOL_HALO + TILE)
    pos = (pos0 + 1 + lax.broadcasted_iota(jnp.int32, (TILE, POOL_CH), 0)).astype(F32)
    groups = []
    for g, w in enumerate(POOL_WINDOWS):
        cols = slice(g * POOL_CH, (g + 1) * POOL_CH)
        s = ext[:, cols]
        d = 1
        while d < w:
            pool_ref[body, cols] = s
            s = s + pool_ref[POOL_HALO - d:2 * POOL_HALO + TILE - d, cols]
            d *= 2
        tok = u[:, cols]
        pooled = s[POOL_HALO:] / jnp.minimum(pos, float(w)) - tok
        groups.append(_dot(pooled.astype(BF16), pw_ref[g]))
        yield
    mixed = (jnp.concatenate(groups, axis=-1) * ps_ref[...]).astype(BF16)
    y = []
    for i, c in enumerate(col_pieces):
        yb = _dot(mixed, wb_ref[:, c])
        y.append((src["ga"][:, c] * ya[i] + src["gb"][:, c] * yb).astype(BF16))
        yield
    y = jnp.concatenate(y, axis=-1)
    for c in col_pieces:
        out_ref[rows, c] = x_ref[rows, c] + _dot(y, wo_ref[:, c])
        yield


MERGE_PIECES = HEADS // 2 + len(POOL_WINDOWS) + 3 * (D_MODEL // MERGE_COLS)


def _interleave(pipes):
    done = [0] * len(pipes)
    alive = [True] * len(pipes)
    while any(alive):
        i = min((i for i in range(len(pipes)) if alive[i]), key=lambda i: (done[i] + 1) / pipes[i][1])
        try:
            next(pipes[i][0])
            done[i] += 1
        except StopIteration:
            alive[i] = False


def _mixer_kernel(xp_ref, xm_ref, g_ref, w_ref, lb_ref, sums_ref, masks_ref, onorm_ref, wa_ref, pw_ref, ps_ref,
                  wb_ref, wo_ref, *rest, tiles_per_seq, n_cast):
    out_ref = rest[n_cast]
    _cast_slabs(rest[:n_cast], rest[n_cast + 1:2 * n_cast + 1])
    scratch = list(rest[2 * n_cast + 1:])

    def take(names):
        return {n: scratch.pop(0) for n in names}

    proj_buf = [take(PROJ_F32 + PROJ_BF16 + ("u",)) for _ in range(2)]
    merge_buf = [take(MERGE_NAMES) for _ in range(2)]
    o_buf = [scratch.pop(0) for _ in range(2)]
    st_ref, pool_ref, halo_ref = scratch
    i = pl.program_id(0)

    @pl.when(i == 0)
    def _():
        for ref in list(proj_buf[1].values()) + list(merge_buf[0].values()) + [o_buf[0], st_ref, halo_ref, pool_ref]:
            ref[...] = jnp.zeros_like(ref)

    for a in range(2):
        g = 2 * i + a
        rows = slice(a * TILE, (a + 1) * TILE)
        seq_h = lax.rem(g - 1 + tiles_per_seq, tiles_per_seq)
        seq_m = lax.rem(g - 2 + tiles_per_seq, tiles_per_seq)
        _interleave([
            (_proj_pipe(xp_ref[rows, :], g_ref, w_ref, lb_ref, proj_buf[a]), PROJ_PIECES),
            (_hgrn_pipe(proj_buf[1 - a], o_buf[1 - a], merge_buf[1 - a], sums_ref, masks_ref, st_ref,
                        seq_h == 0), HGRN_PIECES),
            (_merge_pipe(xm_ref, rows, o_buf[a], merge_buf[a], onorm_ref, wa_ref, pw_ref, ps_ref, wb_ref, wo_ref,
                         out_ref, pool_ref, halo_ref, seq_m == 0, seq_m * TILE), MERGE_PIECES),
        ])


def _mixer(x, g, w_in, hgrn_lb, onorm, wa, pw, ps, wb, wo, seq, cast=()):
    t = x.shape[0]
    n_steps = t // PAIR + 1
    cast_in, cast_out, cast_shapes = _cast_specs(cast, n_steps)
    sums, masks = _level_tables()
    sums2 = jnp.asarray(np.concatenate([sums, sums], axis=1), BF16)
    tile = lambda width, dtype: pltpu.VMEM((TILE, width), dtype)
    proj_slot = ([tile(HG_WIDTH, F32)] * len(PROJ_F32) + [tile(HG_WIDTH, BF16)] * len(PROJ_BF16)
                 + [tile(POOL_WIDTH, BF16)])
    merge_slot = [tile(HG_WIDTH, BF16)] * 3 + [tile(POOL_WIDTH, BF16)]
    return pl.pallas_call(
        functools.partial(_mixer_kernel, tiles_per_seq=seq // TILE, n_cast=len(cast)),
        grid=(n_steps,),
        in_specs=[pl.BlockSpec((PAIR, D_MODEL), lambda i: (jnp.minimum(i, n_steps - 2), 0)),
                  pl.BlockSpec((PAIR, D_MODEL), lambda i: (jnp.maximum(i - 1, 0), 0)),
                  _const_spec((1, D_MODEL)), _const_spec((D_MODEL, IN_SPLITS[-1])), _const_spec(hgrn_lb.shape),
                  _const_spec(sums2.shape), _const_spec(masks.shape),
                  _const_spec((1, HG_WIDTH)), _const_spec((HG_WIDTH, D_MODEL)),
                  _const_spec((len(POOL_WINDOWS), POOL_CH, POOL_CH)), _const_spec((1, POOL_WIDTH)),
                  _const_spec((POOL_WIDTH, D_MODEL)), _const_spec((D_MODEL, D_MODEL))] + cast_in,
        out_specs=[pl.BlockSpec((PAIR, D_MODEL), lambda i: (jnp.maximum(i - 1, 0), 0))] + cast_out,
        out_shape=[jax.ShapeDtypeStruct((t, D_MODEL), F32)] + cast_shapes,
        scratch_shapes=(proj_slot * 2 + merge_slot * 2 + [tile(HG_WIDTH, BF16)] * 2
                        + [pltpu.VMEM((HEADS, HEAD_DIM, HEAD_DIM), F32),
                           pltpu.VMEM((2 * POOL_HALO + TILE, POOL_WIDTH), F32),
                           pltpu.VMEM((POOL_HALO, POOL_WIDTH), F32)]),
        compiler_params=_params(1, sequential=True),
        name="mixer",
    )(x, x, g, w_in, hgrn_lb, sums2, jnp.asarray(masks), onorm, wa, pw, ps, wb, wo, *cast)


def _ffn_ple_kernel(x_ref, p_ref, g_ref, w1_ref, w3_ref, w2_ref, gn_ref, wg_ref, wp_ref, pn_ref, fn_ref, o_ref):
    x = x_ref[...]
    e = _rms(_dot(p_ref[...].astype(BF16), wp_ref[...]), pn_ref[...])
    h = _rms(x, g_ref[...]).astype(BF16)
    act = (_silu(_dot(h, w1_ref[...])) * _dot(h, w3_ref[...])).astype(BF16)
    half = x.shape[0] // 2
    halves = (slice(0, half), slice(half, 2 * half))
    xs = [x[r] + 0.5 * _dot(act[r], w2_ref[...]) for r in halves]
    for r, xh in zip(halves, xs):
        gate = _sigmoid(_dot(_rms(xh, gn_ref[...]).astype(BF16), wg_ref[...]))
        o_ref[r, :] = _rms(xh + gate * e[r], fn_ref[...])


def _ffn_ple(x, p, g, w1, w3, w2, gn, wg, wp, pn, fn, tm=512):
    t = x.shape[0]
    row = pl.BlockSpec((tm, D_MODEL), lambda i: (i, 0))
    vec = _const_spec((1, D_MODEL))
    return pl.pallas_call(
        _ffn_ple_kernel,
        grid=(t // tm,),
        in_specs=[row, pl.BlockSpec((tm, PLE_DIM), lambda i: (i, 0)), vec, _const_spec((D_MODEL, D_FF)),
                  _const_spec((D_MODEL, D_FF)), _const_spec((D_FF, D_MODEL)), vec,
                  _const_spec((D_MODEL, D_MODEL)), _const_spec((PLE_DIM, D_MODEL)), vec, vec],
        out_specs=row,
        out_shape=jax.ShapeDtypeStruct((t, D_MODEL), F32),
        compiler_params=_params(1),
        name="ffn_ple",
    )(x, p, g, w1, w3, w2, gn, wg, wp, pn, fn)


def kernel(x, p, ffn1_norm, ffn1_w1, ffn1_w3, ffn1_w2, mix_norm, w_in, hgrn_lb, hgrn_onorm, w_branch_a, pool_w, pool_scale, w_branch_b, w_out, ffn2_norm, ffn2_w1, ffn2_w3, ffn2_w2, ple_norm, ple_w_gate, ple_w_proj, ple_post_norm, final_norm):
    batch, seq, d = x.shape
    assert d == D_MODEL and p.shape[0] == 1 and hgrn_lb.shape == (2, HG_WIDTH)
    assert seq % PAIR == 0
    t = batch * seq
    bf = lambda w: w.astype(BF16)
    vec = lambda g: g.reshape(1, -1)

    xt = x.reshape(t, d)
    xt, w_in_b, wa, wo, wb, pw = _ffn(
        xt, vec(ffn1_norm[0]), bf(ffn1_w1[0]), bf(ffn1_w3[0]), bf(ffn1_w2[0]),
        cast=(w_in[0], w_branch_a[0], w_out[0], w_branch_b[0], pool_w[0].reshape(POOL_WIDTH, POOL_CH)))
    xt, w1, w3, w2, wg, wp = _mixer(
        xt, vec(mix_norm[0]), w_in_b, hgrn_lb, vec(hgrn_onorm[0]), wa, pw.reshape(pool_w.shape[1:]),
        vec(pool_scale[0]), wb, wo, seq, cast=(ffn2_w1[0], ffn2_w3[0], ffn2_w2[0], ple_w_gate[0], ple_w_proj[0]))
    out = _ffn_ple(xt, p[0].reshape(t, PLE_DIM), vec(ffn2_norm[0]), w1, w3, w2,
                   vec(ple_norm[0]), wg, wp, vec(ple_post_norm[0]), vec(final_norm))
    return out.reshape(batch, seq, d)
```

```python
import functools

import numpy as np
import jax
import jax.numpy as jnp
from jax import lax
from jax.experimental import pallas as pl
from jax.experimental.pallas import tpu as pltpu

D_MODEL = 1024
D_FF = 2816
PLE_DIM = 256
HEADS = 8
HEAD_DIM = 128
HG_WIDTH = HEADS * HEAD_DIM
POOL_WINDOWS = (2, 4, 8, 16)
POOL_CH = 128
POOL_WIDTH = len(POOL_WINDOWS) * POOL_CH
POOL_HALO = 16
IN_SPLITS = (0, 1024, 2048, 3072, 4096, 4608, 5632, 6656)
EPS = 1e-6
LOG2E = 1.4426950408889634

CHUNK = 64
FINE_HALVES = (2, 4)
LEVEL_HALVES = FINE_HALVES + (8, 16, 32)
SLAB = 16
SUBLANES = 8
LANES = 128
DIAG_MASK = len(LEVEL_HALVES)
PAIR_MASK = DIAG_MASK + 1

TILE = 256
PAIR = 2 * TILE
PROJ_COLS = 256
MERGE_COLS = 256

V7X_VMEM_BYTES = 64 * 1024 * 1024
VMEM_LIMIT = V7X_VMEM_BYTES - 8 * 1024 * 1024

BF16 = jnp.bfloat16
F32 = jnp.float32


def _rms(x, g):
    return x * lax.rsqrt(jnp.mean(x * x, axis=-1, keepdims=True) + EPS) * g


def _silu(x):
    return x * jax.nn.sigmoid(x)


def _dot(a, b):
    return jnp.dot(a, b, preferred_element_type=F32)


def _dot_nt(a, b):
    return lax.dot_general(a, b, (((1,), (1,)), ((), ())), preferred_element_type=F32)


def _const_spec(shape):
    zeros = (0,) * len(shape)
    return pl.BlockSpec(shape, lambda *_: zeros, pipeline_mode=pl.Buffered(1))


def _params(n_grid_dims, sequential=False):
    sem = ("arbitrary" if sequential else "parallel",) * n_grid_dims
    return pltpu.CompilerParams(dimension_semantics=sem, vmem_limit_bytes=VMEM_LIMIT)


def _cast_specs(weights, n_steps):
    in_specs, out_specs, out_shapes = [], [], []
    for w in weights:
        rows, cols = w.shape
        slab = next(s for s in range(SLAB, rows + 1, SLAB) if rows % s == 0 and s * n_steps >= rows)
        last = rows // slab - 1
        index = lambda i, last=last: (jnp.minimum(i, last), 0)
        in_specs.append(pl.BlockSpec((slab, cols), index))
        out_specs.append(pl.BlockSpec((slab, cols), index))
        out_shapes.append(jax.ShapeDtypeStruct((rows, cols), BF16))
    return in_specs, out_specs, out_shapes


def _cast_slabs(src_refs, dst_refs):
    for src, dst in zip(src_refs, dst_refs):
        dst[...] = src[...].astype(BF16)


def _ffn_kernel(x_ref, g_ref, w1_ref, w3_ref, w2_ref, *rest):
    n_cast = len(rest) // 2
    o_ref = rest[n_cast]
    _cast_slabs(rest[:n_cast], rest[n_cast + 1:])
    x = x_ref[...]
    h = _rms(x, g_ref[...]).astype(BF16)
    a = _dot(h, w1_ref[...])
    b = _dot(h, w3_ref[...])
    act = (_silu(a) * b).astype(BF16)
    o_ref[...] = x + 0.5 * _dot(act, w2_ref[...])


def _ffn(x, g, w1, w3, w2, cast=(), tm=512):
    t = x.shape[0]
    row = pl.BlockSpec((tm, D_MODEL), lambda i: (i, 0))
    cast_in, cast_out, cast_shapes = _cast_specs(cast, t // tm)
    return pl.pallas_call(
        _ffn_kernel,
        grid=(t // tm,),
        in_specs=[row, _const_spec((1, D_MODEL)), _const_spec((D_MODEL, D_FF)),
                  _const_spec((D_MODEL, D_FF)), _const_spec((D_FF, D_MODEL))] + cast_in,
        out_specs=[row] + cast_out,
        out_shape=[jax.ShapeDtypeStruct((t, D_MODEL), F32)] + cast_shapes,
        compiler_params=_params(1),
        name="ffn",
    )(x, g, w1, w3, w2, *cast)


def _level_tables():
    c = CHUNK
    sums = np.zeros((len(FINE_HALVES) + 1, c, c), np.float32)
    masks = np.zeros((len(LEVEL_HALVES) + 2, c, c), np.float32)
    masks[DIAG_MASK] = np.eye(c)
    masks[PAIR_MASK, 1::2] = np.eye(c)[0::2]
    for li, b in enumerate(LEVEL_HALVES):
        for t in range(c):
            base = (t // (2 * b)) * 2 * b
            r = base + b - 1
            if t > r:
                masks[li, t, base:r + 1] = 1.0
            if b in FINE_HALVES:
                sums[li, t, (r + 1 if t > r else t + 1):(t + 1 if t > r else r + 1)] = 1.0
    for t in range(c):
        sums[len(FINE_HALVES), t, :t + 1] = 1.0
    return sums.reshape(-1, c), masks


PROJ_F32 = ("q", "k")
PROJ_BF16 = ("v", "lfh", "lfl", "og", "ga", "gb")
MERGE_NAMES = ("og", "ga", "gb", "u")


def _proj_pipe(x, g_ref, w_ref, lb_ref, dst):
    h = _rms(x, g_ref[...]).astype(BF16)
    lbp = lb_ref[...]
    e = jnp.exp(lbp - jnp.max(lbp, axis=0, keepdims=True))
    lb = e[0:1, :] / jnp.sum(e, axis=0, keepdims=True)
    yield

    def pieces(i):
        for lo in range(IN_SPLITS[i], IN_SPLITS[i + 1], PROJ_COLS):
            yield slice(lo - IN_SPLITS[i], lo - IN_SPLITS[i] + PROJ_COLS), _dot(h, w_ref[:, lo:lo + PROJ_COLS])

    for c, z in pieces(0):
        dst["q"][:, c] = _silu(z)
        yield
    for c, z in pieces(1):
        f = lb[:, c] + (1.0 - lb[:, c]) * jax.nn.sigmoid(z)
        lf = jnp.log(f) * LOG2E
        hi = lf.astype(BF16)
        dst["lfh"][:, c] = hi
        dst["lfl"][:, c] = (lf - hi.astype(F32)).astype(BF16)
        dst["k"][:, c] = 1.0 - f
        yield
    for c, z in pieces(2):
        dst["v"][:, c] = z.astype(BF16)
        yield
    for c, z in pieces(3):
        dst["og"][:, c] = _silu(z).astype(BF16)
        yield
    for c, z in pieces(4):
        dst["u"][:, c] = z.astype(BF16)
        yield
    for name, i in (("ga", 5), ("gb", 6)):
        for c, z in pieces(i):
            dst[name][:, c] = jax.nn.sigmoid(z).astype(BF16)
            yield


PROJ_PIECES = 1 + IN_SPLITS[-1] // PROJ_COLS


def _hgrn_pipe(src, o_dst, merge_dst, sums_ref, masks_ref, st_ref, new_sequence):
    for name in MERGE_NAMES:
        merge_dst[name][...] = src[name][...]
    yield

    row = lax.broadcasted_iota(jnp.int32, (CHUNK, 1), 0)
    head_cols = [slice(h * HEAD_DIM, (h + 1) * HEAD_DIM) for h in range(HEADS)]
    n_chunks = TILE // CHUNK

    def head_sums(x):
        return [jnp.sum(x[:, hc], axis=-1, keepdims=True) for hc in head_cols]

    def prev_row(x):
        return jnp.concatenate([pltpu.roll(x[r:r + SUBLANES], 1, 0) for r in range(0, CHUNK, SUBLANES)], axis=0)

    def intra(c):
        rows = slice(c * CHUNK, (c + 1) * CHUNK)
        q = src["q"][rows, :]
        k = src["k"][rows, :]
        vb = src["v"][rows, :]
        expo = _dot(sums_ref[...], jnp.concatenate([src["lfh"][rows, :], src["lfl"][rows, :]], axis=0))
        g = expo[len(FINE_HALVES) * CHUNK:, :]

        diag = head_sums(q * k)
        pair = head_sums(q * (1.0 - k) * prev_row(k))
        yield

        n_groups = CHUNK // SUBLANES
        groups = [[None] * n_groups for _ in range(HEADS)]

        def add(h, r0, val):
            for t in range(0, val.shape[0], SUBLANES):
                i = (r0 + t) // SUBLANES
                piece = val[t:t + SUBLANES]
                groups[h][i] = piece if groups[h][i] is None else groups[h][i] + piece

        for li, b in enumerate(LEVEL_HALVES):
            m = masks_ref[li]
            if b in FINE_HALVES:
                qk = jnp.where((row % (2 * b)) >= b, q, k)
                x = (qk * jnp.exp2(expo[li * CHUNK:(li + 1) * CHUNK, :])).astype(BF16)
                for h, hc in enumerate(head_cols):
                    add(h, 0, m * _dot_nt(x[:, hc], x[:, hc]))
            else:
                queries, keys = [], []
                for r0 in range(0, CHUNK, b):
                    r = (r0 // (2 * b)) * 2 * b + b - 1
                    if (r0 // b) % 2:
                        queries.append((r0, q[r0:r0 + b] * jnp.exp2(g[r0:r0 + b] - g[r:r + 1])))
                    else:
                        keys.append((r0, k[r0:r0 + b] * jnp.exp2(g[r:r + 1] - g[r0:r0 + b])))
                lhs = jnp.concatenate([p for _, p in queries], axis=0).astype(BF16)
                rhs = jnp.concatenate([p for _, p in keys], axis=0).astype(BF16)
                key_at = {r0: i * b for i, (r0, _) in enumerate(keys)}
                fill = jnp.zeros((lhs.shape[0], LANES - rhs.shape[0]), F32)
                for h, hc in enumerate(head_cols):
                    s = jnp.concatenate([_dot_nt(lhs[:, hc], rhs[:, hc]), fill], axis=1)
                    for i, (r0, _) in enumerate(queries):
                        blk = s[i * b:(i + 1) * b]
                        first_key = (r0 // (2 * b)) * 2 * b
                        if first_key != key_at[first_key]:
                            blk = pltpu.roll(blk, first_key - key_at[first_key], 1)
                        blk = blk[:, :CHUNK]
                        add(h, r0, blk if 2 * b == CHUNK else m[r0:r0 + b] * blk)
            yield
        q_state = (q * jnp.exp2(g)).astype(BF16)
        k_end = k * jnp.exp2(g[CHUNK - 1:CHUNK] - g)
        chunk_decay = jnp.exp2(g[CHUNK - 1:CHUNK])
        o_intra, k_end_t, decay_col = [], [], []
        for h, hc in enumerate(head_cols):
            scores = jnp.concatenate(groups[h], axis=0)
            scores = scores + diag[h] * masks_ref[DIAG_MASK] + pair[h] * masks_ref[PAIR_MASK]
            o_intra.append(_dot(scores.astype(BF16), vb[:, hc]))
            k_end_t.append(k_end[:, hc].T.astype(BF16))
            decay_col.append(jnp.broadcast_to(chunk_decay[:, hc], (SUBLANES, HEAD_DIM)).T[:, 0:1])
        yield
        return o_intra, q_state, k_end_t, decay_col, vb

    def inter(c, o_intra, q_state, k_end_t, decay_col, vb):
        rows = slice(c * CHUNK, (c + 1) * CHUNK)
        states = [st_ref[h] for h in range(HEADS)]
        if c == 0:
            states = [jnp.where(new_sequence, 0.0, s) for s in states]
        for h, hc in enumerate(head_cols):
            o_dst[rows, hc] = (o_intra[h] + _dot(q_state[:, hc], states[h].astype(BF16))).astype(BF16)
        for h, hc in enumerate(head_cols):
            st_ref[h] = decay_col[h] * states[h] + _dot(k_end_t[h], vb[:, hc])

    pending = yield from intra(0)
    for c in range(n_chunks):
        nxt = (yield from intra(c + 1)) if c + 1 < n_chunks else None
        inter(c, *pending)
        yield
        pending = nxt


HGRN_PIECES = 1 + (TILE // CHUNK) * (len(LEVEL_HALVES) + 3)


def _merge_pipe(x_ref, rows, o_src, src, onorm_ref, wa_ref, pw_ref, ps_ref, wb_ref, wo_ref, out_ref,
                pool_ref, halo_ref, new_sequence, pos0):
    col_pieces = [slice(lo, lo + MERGE_COLS) for lo in range(0, D_MODEL, MERGE_COLS)]
    o = o_src[...].astype(F32)
    heads = []
    for h in range(HEADS):
        oh = o[:, h * HEAD_DIM:(h + 1) * HEAD_DIM]
        heads.append(oh * lax.rsqrt(jnp.mean(oh * oh, axis=-1, keepdims=True) + EPS))
        if h % 2:
            yield
    on = (jnp.concatenate(heads, axis=-1) * onorm_ref[...] * src["og"][...]).astype(BF16)
    ya = []
    for c in col_pieces:
        ya.append(_dot(on, wa_ref[:, c]))
        yield

    u = src["u"][...].astype(F32)
    ext = jnp.concatenate([jnp.where(new_sequence, 0.0, halo_ref[...]), u], axis=0)
    halo_ref[...] = u[TILE - POOL_HALO:TILE, :]
    body = slice(POOL_HALO, 2 * POOL_HALO + TILE)
    pos = (pos0 + 1 + lax.broadcasted_iota(jnp.int32, (TILE, POOL_CH), 0)).astype(F32)
    groups = []
    for g, w in enumerate(POOL_WINDOWS):
        cols = slice(g * POOL_CH, (g + 1) * POOL_CH)
        s = ext[:, cols]
        d = 1
        while d < w:
            pool_ref[body, cols] = s
            s = s + pool_ref[POOL_HALO - d:2 * POOL_HALO + TILE - d, cols]
            d *= 2
        tok = u[:, cols]
        pooled = s[POOL_HALO:] / jnp.minimum(pos, float(w)) - tok
        groups.append(_dot(pooled.astype(BF16), pw_ref[g]))
        yield
    mixed = (jnp.concatenate(groups, axis=-1) * ps_ref[...]).astype(BF16)
    y = []
    for i, c in enumerate(col_pieces):
        yb = _dot(mixed, wb_ref[:, c])
        y.append((src["ga"][:, c] * ya[i] + src["gb"][:, c] * yb).astype(BF16))
        yield
    y = jnp.concatenate(y, axis=-1)
    for c in col_pieces:
        out_ref[rows, c] = x_ref[rows, c] + _dot(y, wo_ref[:, c])
        yield


MERGE_PIECES = HEADS // 2 + len(POOL_WINDOWS) + 3 * (D_MODEL // MERGE_COLS)


def _interleave(pipes):
    done = [0] * len(pipes)
    alive = [True] * len(pipes)
    while any(alive):
        i = min((i for i in range(len(pipes)) if alive[i]), key=lambda i: (done[i] + 1) / pipes[i][1])
        try:
            next(pipes[i][0])
            done[i] += 1
        except StopIteration:
            alive[i] = False


def _mixer_kernel(xp_ref, xm_ref, g_ref, w_ref, lb_ref, sums_ref, masks_ref, onorm_ref, wa_ref, pw_ref, ps_ref,
                  wb_ref, wo_ref, *rest, tiles_per_seq, n_cast):
    out_ref = rest[n_cast]
    _cast_slabs(rest[:n_cast], rest[n_cast + 1:2 * n_cast + 1])
    scratch = list(rest[2 * n_cast + 1:])

    def take(names):
        return {n: scratch.pop(0) for n in names}

    proj_buf = [take(PROJ_F32 + PROJ_BF16 + ("u",)) for _ in range(2)]
    merge_buf = [take(MERGE_NAMES) for _ in range(2)]
    o_buf = [scratch.pop(0) for _ in range(2)]
    st_ref, pool_ref, halo_ref = scratch
    i = pl.program_id(0)

    @pl.when(i == 0)
    def _():
        for ref in list(proj_buf[1].values()) + list(merge_buf[0].values()) + [o_buf[0], st_ref, halo_ref, pool_ref]:
            ref[...] = jnp.zeros_like(ref)

    for a in range(2):
        g = 2 * i + a
        rows = slice(a * TILE, (a + 1) * TILE)
        seq_h = lax.rem(g - 1 + tiles_per_seq, tiles_per_seq)
        seq_m = lax.rem(g - 2 + tiles_per_seq, tiles_per_seq)
        _interleave([
            (_proj_pipe(xp_ref[rows, :], g_ref, w_ref, lb_ref, proj_buf[a]), PROJ_PIECES),
            (_hgrn_pipe(proj_buf[1 - a], o_buf[1 - a], merge_buf[1 - a], sums_ref, masks_ref, st_ref,
                        seq_h == 0), HGRN_PIECES),
            (_merge_pipe(xm_ref, rows, o_buf[a], merge_buf[a], onorm_ref, wa_ref, pw_ref, ps_ref, wb_ref, wo_ref,
                         out_ref, pool_ref, halo_ref, seq_m == 0, seq_m * TILE), MERGE_PIECES),
        ])


def _mixer(x, g, w_in, hgrn_lb, onorm, wa, pw, ps, wb, wo, seq, cast=()):
    t = x.shape[0]
    n_steps = t // PAIR + 1
    cast_in, cast_out, cast_shapes = _cast_specs(cast, n_steps)
    sums, masks = _level_tables()
    sums2 = jnp.asarray(np.concatenate([sums, sums], axis=1), BF16)
    tile = lambda width, dtype: pltpu.VMEM((TILE, width), dtype)
    proj_slot = ([tile(HG_WIDTH, F32)] * len(PROJ_F32) + [tile(HG_WIDTH, BF16)] * len(PROJ_BF16)
                 + [tile(POOL_WIDTH, BF16)])
    merge_slot = [tile(HG_WIDTH, BF16)] * 3 + [tile(POOL_WIDTH, BF16)]
    return pl.pallas_call(
        functools.partial(_mixer_kernel, tiles_per_seq=seq // TILE, n_cast=len(cast)),
        grid=(n_steps,),
        in_specs=[pl.BlockSpec((PAIR, D_MODEL), lambda i: (jnp.minimum(i, n_steps - 2), 0)),
                  pl.BlockSpec((PAIR, D_MODEL), lambda i: (jnp.maximum(i - 1, 0), 0)),
                  _const_spec((1, D_MODEL)), _const_spec((D_MODEL, IN_SPLITS[-1])), _const_spec(hgrn_lb.shape),
                  _const_spec(sums2.shape), _const_spec(masks.shape),
                  _const_spec((1, HG_WIDTH)), _const_spec((HG_WIDTH, D_MODEL)),
                  _const_spec((len(POOL_WINDOWS), POOL_CH, POOL_CH)), _const_spec((1, POOL_WIDTH)),
                  _const_spec((POOL_WIDTH, D_MODEL)), _const_spec((D_MODEL, D_MODEL))] + cast_in,
        out_specs=[pl.BlockSpec((PAIR, D_MODEL), lambda i: (jnp.maximum(i - 1, 0), 0))] + cast_out,
        out_shape=[jax.ShapeDtypeStruct((t, D_MODEL), F32)] + cast_shapes,
        scratch_shapes=(proj_slot * 2 + merge_slot * 2 + [tile(HG_WIDTH, BF16)] * 2
                        + [pltpu.VMEM((HEADS, HEAD_DIM, HEAD_DIM), F32),
                           pltpu.VMEM((2 * POOL_HALO + TILE, POOL_WIDTH), F32),
                           pltpu.VMEM((POOL_HALO, POOL_WIDTH), F32)]),
        compiler_params=_params(1, sequential=True),
        name="mixer",
    )(x, x, g, w_in, hgrn_lb, sums2, jnp.asarray(masks), onorm, wa, pw, ps, wb, wo, *cast)


def _ffn_ple_kernel(x_ref, p_ref, g_ref, w1_ref, w3_ref, w2_ref, gn_ref, wg_ref, wp_ref, pn_ref, fn_ref, o_ref):
    x = x_ref[...]
    e = _rms(_dot(p_ref[...].astype(BF16), wp_ref[...]), pn_ref[...])
    h = _rms(x, g_ref[...]).astype(BF16)
    act = (_silu(_dot(h, w1_ref[...])) * _dot(h, w3_ref[...])).astype(BF16)
    half = x.shape[0] // 2
    halves = (slice(0, half), slice(half, 2 * half))
    xs = [x[r] + 0.5 * _dot(act[r], w2_ref[...]) for r in halves]
    for r, xh in zip(halves, xs):
        gate = jax.nn.sigmoid(_dot(_rms(xh, gn_ref[...]).astype(BF16), wg_ref[...]))
        o_ref[r, :] = _rms(xh + gate * e[r], fn_ref[...])


def _ffn_ple(x, p, g, w1, w3, w2, gn, wg, wp, pn, fn, tm=512):
    t = x.shape[0]
    row = pl.BlockSpec((tm, D_MODEL), lambda i: (i, 0))
    vec = _const_spec((1, D_MODEL))
    return pl.pallas_call(
        _ffn_ple_kernel,
        grid=(t // tm,),
        in_specs=[row, pl.BlockSpec((tm, PLE_DIM), lambda i: (i, 0)), vec, _const_spec((D_MODEL, D_FF)),
                  _const_spec((D_MODEL, D_FF)), _const_spec((D_FF, D_MODEL)), vec,
                  _const_spec((D_MODEL, D_MODEL)), _const_spec((PLE_DIM, D_MODEL)), vec, vec],
        out_specs=row,
        out_shape=jax.ShapeDtypeStruct((t, D_MODEL), F32),
        compiler_params=_params(1),
        name="ffn_ple",
    )(x, p, g, w1, w3, w2, gn, wg, wp, pn, fn)


def kernel(x, p, ffn1_norm, ffn1_w1, ffn1_w3, ffn1_w2, mix_norm, w_in, hgrn_lb, hgrn_onorm, w_branch_a, pool_w, pool_scale, w_branch_b, w_out, ffn2_norm, ffn2_w1, ffn2_w3, ffn2_w2, ple_norm, ple_w_gate, ple_w_proj, ple_post_norm, final_norm):
    batch, seq, d = x.shape
    assert d == D_MODEL and p.shape[0] == 1 and hgrn_lb.shape == (2, HG_WIDTH)
    assert seq % PAIR == 0
    t = batch * seq
    bf = lambda w: w.astype(BF16)
    vec = lambda g: g.reshape(1, -1)

    xt = x.reshape(t, d)
    xt, w_in_b, wa, wo, wb, pw = _ffn(
        xt, vec(ffn1_norm[0]), bf(ffn1_w1[0]), bf(ffn1_w3[0]), bf(ffn1_w2[0]),
        cast=(w_in[0], w_branch_a[0], w_out[0], w_branch_b[0], pool_w[0].reshape(POOL_WIDTH, POOL_CH)))
    xt, w1, w3, w2, wg, wp = _mixer(
        xt, vec(mix_norm[0]), w_in_b, hgrn_lb, vec(hgrn_onorm[0]), wa, pw.reshape(pool_w.shape[1:]),
        vec(pool_scale[0]), wb, wo, seq, cast=(ffn2_w1[0], ffn2_w3[0], ffn2_w2[0], ple_w_gate[0], ple_w_proj[0]))
    out = _ffn_ple(xt, p[0].reshape(t, PLE_DIM), vec(ffn2_norm[0]), w1, w3, w2,
                   vec(ple_norm[0]), wg, wp, vec(ple_post_norm[0]), vec(final_norm))
    return out.reshape(batch, seq, d)
```

```python
import functools

import numpy as np
import jax
import jax.numpy as jnp
from jax import lax
from jax.experimental import pallas as pl
from jax.experimental.pallas import tpu as pltpu

D_MODEL = 1024
D_FF = 2816
PLE_DIM = 256
HEADS = 8
HEAD_DIM = 128
HG_WIDTH = HEADS * HEAD_DIM
POOL_WINDOWS = (2, 4, 8, 16)
POOL_CH = 128
POOL_WIDTH = len(POOL_WINDOWS) * POOL_CH
POOL_HALO = 16
IN_SPLITS = (0, 1024, 2048, 3072, 4096, 4608, 5632, 6656)
EPS = 1e-6
LOG2E = 1.4426950408889634

CHUNK = 64
FINE_HALVES = (2, 4)
LEVEL_HALVES = FINE_HALVES + (8, 16, 32)
SLAB = 16
SUBLANES = 8
LANES = 128
DIAG_MASK = len(LEVEL_HALVES)
PAIR_MASK = DIAG_MASK + 1

TILE = 256
PAIR = 2 * TILE
PROJ_COLS = 256
MERGE_COLS = 256

V7X_VMEM_BYTES = 64 * 1024 * 1024
VMEM_LIMIT = V7X_VMEM_BYTES - 8 * 1024 * 1024

BF16 = jnp.bfloat16
F32 = jnp.float32


def _rms(x, g):
    return x * lax.rsqrt(jnp.mean(x * x, axis=-1, keepdims=True) + EPS) * g


def _silu(x):
    return x * jax.nn.sigmoid(x)


def _dot(a, b):
    return jnp.dot(a, b, preferred_element_type=F32)


def _dot_nt(a, b):
    return lax.dot_general(a, b, (((1,), (1,)), ((), ())), preferred_element_type=F32)


def _const_spec(shape):
    zeros = (0,) * len(shape)
    return pl.BlockSpec(shape, lambda *_: zeros, pipeline_mode=pl.Buffered(1))


def _params(n_grid_dims, sequential=False):
    sem = ("arbitrary" if sequential else "parallel",) * n_grid_dims
    return pltpu.CompilerParams(dimension_semantics=sem, vmem_limit_bytes=VMEM_LIMIT)


def _cast_specs(weights, n_steps):
    in_specs, out_specs, out_shapes = [], [], []
    for w in weights:
        rows, cols = w.shape
        slab = next(s for s in range(SLAB, rows + 1, SLAB) if rows % s == 0 and s * n_steps >= rows)
        last = rows // slab - 1
        index = lambda i, last=last: (jnp.minimum(i, last), 0)
        in_specs.append(pl.BlockSpec((slab, cols), index))
        out_specs.append(pl.BlockSpec((slab, cols), index))
        out_shapes.append(jax.ShapeDtypeStruct((rows, cols), BF16))
    return in_specs, out_specs, out_shapes


def _cast_slabs(src_refs, dst_refs):
    for src, dst in zip(src_refs, dst_refs):
        dst[...] = src[...].astype(BF16)


def _ffn_kernel(x_ref, g_ref, w1_ref, w3_ref, w2_ref, *rest):
    n_cast = len(rest) // 2
    o_ref = rest[n_cast]
    _cast_slabs(rest[:n_cast], rest[n_cast + 1:])
    x = x_ref[...]
    h = _rms(x, g_ref[...]).astype(BF16)
    a = _dot(h, w1_ref[...])
    b = _dot(h, w3_ref[...])
    act = (_silu(a) * b).astype(BF16)
    o_ref[...] = x + 0.5 * _dot(act, w2_ref[...])


def _ffn(x, g, w1, w3, w2, cast=(), tm=512):
    t = x.shape[0]
    row = pl.BlockSpec((tm, D_MODEL), lambda i: (i, 0))
    cast_in, cast_out, cast_shapes = _cast_specs(cast, t // tm)
    return pl.pallas_call(
        _ffn_kernel,
        grid=(t // tm,),
        in_specs=[row, _const_spec((1, D_MODEL)), _const_spec((D_MODEL, D_FF)),
                  _const_spec((D_MODEL, D_FF)), _const_spec((D_FF, D_MODEL))] + cast_in,
        out_specs=[row] + cast_out,
        out_shape=[jax.ShapeDtypeStruct((t, D_MODEL), F32)] + cast_shapes,
        compiler_params=_params(1),
        name="ffn",
    )(x, g, w1, w3, w2, *cast)


def _level_tables():
    c = CHUNK
    sums = np.zeros((len(FINE_HALVES) + 1, c, c), np.float32)
    masks = np.zeros((len(LEVEL_HALVES) + 2, c, c), np.float32)
    masks[DIAG_MASK] = np.eye(c)
    masks[PAIR_MASK, 1::2] = np.eye(c)[0::2]
    for li, b in enumerate(LEVEL_HALVES):
        for t in range(c):
            base = (t // (2 * b)) * 2 * b
            r = base + b - 1
            if t > r:
                masks[li, t, base:r + 1] = 1.0
            if b in FINE_HALVES:
                sums[li, t, (r + 1 if t > r else t + 1):(t + 1 if t > r else r + 1)] = 1.0
    for t in range(c):
        sums[len(FINE_HALVES), t, :t + 1] = 1.0
    return sums.reshape(-1, c), masks


PROJ_F32 = ("q", "k")
PROJ_BF16 = ("v", "lfh", "lfl", "og", "ga", "gb")
MERGE_NAMES = ("og", "ga", "gb", "u")


def _proj_pipe(x, g_ref, w_ref, lb_ref, dst):
    h = _rms(x, g_ref[...]).astype(BF16)
    lbp = lb_ref[...]
    e = jnp.exp(lbp - jnp.max(lbp, axis=0, keepdims=True))
    lb = e[0:1, :] / jnp.sum(e, axis=0, keepdims=True)
    yield

    def pieces(i):
        for lo in range(IN_SPLITS[i], IN_SPLITS[i + 1], PROJ_COLS):
            yield slice(lo - IN_SPLITS[i], lo - IN_SPLITS[i] + PROJ_COLS), _dot(h, w_ref[:, lo:lo + PROJ_COLS])

    for c, z in pieces(0):
        dst["q"][:, c] = _silu(z)
        yield
    for c, z in pieces(1):
        f = lb[:, c] + (1.0 - lb[:, c]) * jax.nn.sigmoid(z)
        lf = jnp.log(f) * LOG2E
        hi = lf.astype(BF16)
        dst["lfh"][:, c] = hi
        dst["lfl"][:, c] = (lf - hi.astype(F32)).astype(BF16)
        dst["k"][:, c] = 1.0 - f
        yield
    for c, z in pieces(2):
        dst["v"][:, c] = z.astype(BF16)
        yield
    for c, z in pieces(3):
        dst["og"][:, c] = _silu(z).astype(BF16)
        yield
    for c, z in pieces(4):
        dst["u"][:, c] = z.astype(BF16)
        yield
    for name, i in (("ga", 5), ("gb", 6)):
        for c, z in pieces(i):
            dst[name][:, c] = jax.nn.sigmoid(z).astype(BF16)
            yield


PROJ_PIECES = 1 + IN_SPLITS[-1] // PROJ_COLS


def _hgrn_pipe(src, o_dst, merge_dst, sums_ref, masks_ref, st_ref, new_sequence):
    for name in MERGE_NAMES:
        merge_dst[name][...] = src[name][...]
    yield

    row = lax.broadcasted_iota(jnp.int32, (CHUNK, 1), 0)
    head_cols = [slice(h * HEAD_DIM, (h + 1) * HEAD_DIM) for h in range(HEADS)]
    n_chunks = TILE // CHUNK

    def head_sums(x):
        return [jnp.sum(x[:, hc], axis=-1, keepdims=True) for hc in head_cols]

    def prev_row(x):
        return jnp.concatenate([pltpu.roll(x[r:r + SUBLANES], 1, 0) for r in range(0, CHUNK, SUBLANES)], axis=0)

    def intra(c):
        rows = slice(c * CHUNK, (c + 1) * CHUNK)
        q = src["q"][rows, :]
        k = src["k"][rows, :]
        vb = src["v"][rows, :]
        expo = _dot(sums_ref[...], jnp.concatenate([src["lfh"][rows, :], src["lfl"][rows, :]], axis=0))
        g = expo[len(FINE_HALVES) * CHUNK:, :]

        diag = head_sums(q * k)
        pair = head_sums(q * (1.0 - k) * prev_row(k))
        yield

        n_groups = CHUNK // SUBLANES
        groups = [[None] * n_groups for _ in range(HEADS)]

        def add(h, r0, val):
            for t in range(0, val.shape[0], SUBLANES):
                i = (r0 + t) // SUBLANES
                piece = val[t:t + SUBLANES]
                groups[h][i] = piece if groups[h][i] is None else groups[h][i] + piece

        for li, b in enumerate(LEVEL_HALVES):
            m = masks_ref[li]
            if b in FINE_HALVES:
                qk = jnp.where((row % (2 * b)) >= b, q, k)
                x = (qk * jnp.exp2(expo[li * CHUNK:(li + 1) * CHUNK, :])).astype(BF16)
                for h, hc in enumerate(head_cols):
                    add(h, 0, m * _dot_nt(x[:, hc], x[:, hc]))
            else:
                queries, keys = [], []
                for r0 in range(0, CHUNK, b):
                    r = (r0 // (2 * b)) * 2 * b + b - 1
                    if (r0 // b) % 2:
                        queries.append((r0, q[r0:r0 + b] * jnp.exp2(g[r0:r0 + b] - g[r:r + 1])))
                    else:
                        keys.append((r0, k[r0:r0 + b] * jnp.exp2(g[r:r + 1] - g[r0:r0 + b])))
                lhs = jnp.concatenate([p for _, p in queries], axis=0).astype(BF16)
                rhs = jnp.concatenate([p for _, p in keys], axis=0).astype(BF16)
                key_at = {r0: i * b for i, (r0, _) in enumerate(keys)}
                fill = jnp.zeros((lhs.shape[0], LANES - rhs.shape[0]), F32)
                for h, hc in enumerate(head_cols):
                    s = jnp.concatenate([_dot_nt(lhs[:, hc], rhs[:, hc]), fill], axis=1)
                    for i, (r0, _) in enumerate(queries):
                        blk = s[i * b:(i + 1) * b]
                        first_key = (r0 // (2 * b)) * 2 * b
                        if first_key != key_at[first_key]:
                            blk = pltpu.roll(blk, first_key - key_at[first_key], 1)
                        blk = blk[:, :CHUNK]
                        add(h, r0, blk if 2 * b == CHUNK else m[r0:r0 + b] * blk)
            yield
        q_state = (q * jnp.exp2(g)).astype(BF16)
        k_end = k * jnp.exp2(g[CHUNK - 1:CHUNK] - g)
        chunk_decay = jnp.exp2(g[CHUNK - 1:CHUNK])
        o_intra, k_end_t, decay_col = [], [], []
        for h, hc in enumerate(head_cols):
            scores = jnp.concatenate(groups[h], axis=0)
            scores = scores + diag[h] * masks_ref[DIAG_MASK] + pair[h] * masks_ref[PAIR_MASK]
            o_intra.append(_dot(scores.astype(BF16), vb[:, hc]))
            k_end_t.append(k_end[:, hc].T.astype(BF16))
            decay_col.append(jnp.broadcast_to(chunk_decay[:, hc], (SUBLANES, HEAD_DIM)).T[:, 0:1])
        yield
        return o_intra, q_state, k_end_t, decay_col, vb

    def inter(c, o_intra, q_state, k_end_t, decay_col, vb):
        rows = slice(c * CHUNK, (c + 1) * CHUNK)
        states = [st_ref[h] for h in range(HEADS)]
        if c == 0:
            states = [jnp.where(new_sequence, 0.0, s) for s in states]
        for h, hc in enumerate(head_cols):
            o_dst[rows, hc] = (o_intra[h] + _dot(q_state[:, hc], states[h].astype(BF16))).astype(BF16)
        for h, hc in enumerate(head_cols):
            st_ref[h] = decay_col[h] * states[h] + _dot(k_end_t[h], vb[:, hc])

    pending = yield from intra(0)
    for c in range(n_chunks):
        nxt = (yield from intra(c + 1)) if c + 1 < n_chunks else None
        inter(c, *pending)
        yield
        pending = nxt


HGRN_PIECES = 1 + (TILE // CHUNK) * (len(LEVEL_HALVES) + 3)


def _merge_pipe(x_ref, rows, o_src, src, onorm_ref, wa_ref, pw_ref, ps_ref, wb_ref, wo_ref, out_ref,
                pool_ref, halo_ref, new_sequence, pos0):
    col_pieces = [slice(lo, lo + MERGE_COLS) for lo in range(0, D_MODEL, MERGE_COLS)]
    o = o_src[...].astype(F32)
    heads = []
    for h in range(HEADS):
        oh = o[:, h * HEAD_DIM:(h + 1) * HEAD_DIM]
        heads.append(oh * lax.rsqrt(jnp.mean(oh * oh, axis=-1, keepdims=True) + EPS))
        if h % 2:
            yield
    on = (jnp.concatenate(heads, axis=-1) * onorm_ref[...] * src["og"][...]).astype(BF16)
    ya = []
    for c in col_pieces:
        ya.append(_dot(on, wa_ref[:, c]))
        yield

    u = src["u"][...].astype(F32)
    ext = jnp.concatenate([jnp.where(new_sequence, 0.0, halo_ref[...]), u], axis=0)
    halo_ref[...] = u[TILE - POOL_HALO:TILE, :]
    body = slice(POOL_HALO, 2 * POOL_HALO + TILE)
    pos = (pos0 + 1 + lax.broadcasted_iota(jnp.int32, (TILE, POOL_CH), 0)).astype(F32)
    groups = []
    for g, w in enumerate(POOL_WINDOWS):
        cols = slice(g * POOL_CH, (g + 1) * POOL_CH)
        s = ext[:, cols]
        d = 1
        while d < w:
            pool_ref[body, cols] = s
            s = s + pool_ref[POOL_HALO - d:2 * POOL_HALO + TILE - d, cols]
            d *= 2
        tok = u[:, cols]
        pooled = s[POOL_HALO:] / jnp.minimum(pos, float(w)) - tok
        groups.append(_dot(pooled.astype(BF16), pw_ref[g]))
        yield
    mixed = (jnp.concatenate(groups, axis=-1) * ps_ref[...]).astype(BF16)
    y = []
    for i, c in enumerate(col_pieces):
        yb = _dot(mixed, wb_ref[:, c])
        y.append((src["ga"][:, c] * ya[i] + src["gb"][:, c] * yb).astype(BF16))
        yield
    y = jnp.concatenate(y, axis=-1)
    for c in col_pieces:
        out_ref[rows, c] = x_ref[rows, c] + _dot(y, wo_ref[:, c])
        yield


MERGE_PIECES = HEADS // 2 + len(POOL_WINDOWS) + 3 * (D_MODEL // MERGE_COLS)


def _interleave(pipes):
    done = [0] * len(pipes)
    alive = [True] * len(pipes)
    while any(alive):
        i = min((i for i in range(len(pipes)) if alive[i]), key=lambda i: (done[i] + 1) / pipes[i][1])
        try:
            next(pipes[i][0])
            done[i] += 1
        except StopIteration:
            alive[i] = False


def _mixer_kernel(xp_ref, xm_ref, g_ref, w_ref, lb_ref, sums_ref, masks_ref, onorm_ref, wa_ref, pw_ref, ps_ref,
                  wb_ref, wo_ref, *rest, tiles_per_seq, n_cast):
    out_ref = rest[n_cast]
    _cast_slabs(rest[:n_cast], rest[n_cast + 1:2 * n_cast + 1])
    scratch = list(rest[2 * n_cast + 1:])

    def take(names):
        return {n: scratch.pop(0) for n in names}

    proj_buf = [take(PROJ_F32 + PROJ_BF16 + ("u",)) for _ in range(2)]
    merge_buf = [take(MERGE_NAMES) for _ in range(2)]
    o_buf = [scratch.pop(0) for _ in range(2)]
    st_ref, pool_ref, halo_ref = scratch
    i = pl.program_id(0)
    last = pl.num_programs(0) - 1

    def stage(a, proj, hgrn, merge):
        g = 2 * i + a
        rows = slice(a * TILE, (a + 1) * TILE)
        seq_h = lax.rem(g - 1 + tiles_per_seq, tiles_per_seq)
        seq_m = lax.rem(g - 2 + tiles_per_seq, tiles_per_seq)
        pipes = []
        if proj:
            pipes.append((_proj_pipe(xp_ref[rows, :], g_ref, w_ref, lb_ref, proj_buf[a]), PROJ_PIECES))
        if hgrn:
            pipes.append((_hgrn_pipe(proj_buf[1 - a], o_buf[1 - a], merge_buf[1 - a], sums_ref, masks_ref, st_ref,
                                     seq_h == 0), HGRN_PIECES))
        if merge:
            pipes.append((_merge_pipe(xm_ref, rows, o_buf[a], merge_buf[a], onorm_ref, wa_ref, pw_ref, ps_ref,
                                      wb_ref, wo_ref, out_ref, pool_ref, halo_ref, seq_m == 0, seq_m * TILE),
                          MERGE_PIECES))
        _interleave(pipes)

    @pl.when(i == 0)
    def _():
        for ref in [st_ref, halo_ref, pool_ref]:
            ref[...] = jnp.zeros_like(ref)
        stage(0, True, False, False)
        stage(1, True, True, False)

    @pl.when(jnp.logical_and(i > 0, i < last))
    def _():
        stage(0, True, True, True)
        stage(1, True, True, True)

    @pl.when(i == last)
    def _():
        stage(0, False, True, True)
        stage(1, False, False, True)


def _mixer(x, g, w_in, hgrn_lb, onorm, wa, pw, ps, wb, wo, seq, cast=()):
    t = x.shape[0]
    n_steps = t // PAIR + 1
    cast_in, cast_out, cast_shapes = _cast_specs(cast, n_steps)
    sums, masks = _level_tables()
    sums2 = jnp.asarray(np.concatenate([sums, sums], axis=1), BF16)
    tile = lambda width, dtype: pltpu.VMEM((TILE, width), dtype)
    proj_slot = ([tile(HG_WIDTH, F32)] * len(PROJ_F32) + [tile(HG_WIDTH, BF16)] * len(PROJ_BF16)
                 + [tile(POOL_WIDTH, BF16)])
    merge_slot = [tile(HG_WIDTH, BF16)] * 3 + [tile(POOL_WIDTH, BF16)]
    return pl.pallas_call(
        functools.partial(_mixer_kernel, tiles_per_seq=seq // TILE, n_cast=len(cast)),
        grid=(n_steps,),
        in_specs=[pl.BlockSpec((PAIR, D_MODEL), lambda i: (jnp.minimum(i, n_steps - 2), 0)),
                  pl.BlockSpec((PAIR, D_MODEL), lambda i: (jnp.maximum(i - 1, 0), 0)),
                  _const_spec((1, D_MODEL)), _const_spec((D_MODEL, IN_SPLITS[-1])), _const_spec(hgrn_lb.shape),
                  _const_spec(sums2.shape), _const_spec(masks.shape),
                  _const_spec((1, HG_WIDTH)), _const_spec((HG_WIDTH, D_MODEL)),
                  _const_spec((len(POOL_WINDOWS), POOL_CH, POOL_CH)), _const_spec((1, POOL_WIDTH)),
                  _const_spec((POOL_WIDTH, D_MODEL)), _const_spec((D_MODEL, D_MODEL))] + cast_in,
        out_specs=[pl.BlockSpec((PAIR, D_MODEL), lambda i: (jnp.maximum(i - 1, 0), 0))] + cast_out,
        out_shape=[jax.ShapeDtypeStruct((t, D_MODEL), F32)] + cast_shapes,
        scratch_shapes=(proj_slot * 2 + merge_slot * 2 + [tile(HG_WIDTH, BF16)] * 2
                        + [pltpu.VMEM((HEADS, HEAD_DIM, HEAD_DIM), F32),
                           pltpu.VMEM((2 * POOL_HALO + TILE, POOL_WIDTH), F32),
                           pltpu.VMEM((POOL_HALO, POOL_WIDTH), F32)]),
        compiler_params=_params(1, sequential=True),
        name="mixer",
    )(x, x, g, w_in, hgrn_lb, sums2, jnp.asarray(masks), onorm, wa, pw, ps, wb, wo, *cast)


def _ffn_ple_kernel(x_ref, p_ref, g_ref, w1_ref, w3_ref, w2_ref, gn_ref, wg_ref, wp_ref, pn_ref, fn_ref, o_ref):
    x = x_ref[...]
    e = _rms(_dot(p_ref[...].astype(BF16), wp_ref[...]), pn_ref[...])
    h = _rms(x, g_ref[...]).astype(BF16)
    act = (_silu(_dot(h, w1_ref[...])) * _dot(h, w3_ref[...])).astype(BF16)
    half = x.shape[0] // 2
    halves = (slice(0, half), slice(half, 2 * half))
    xs = [x[r] + 0.5 * _dot(act[r], w2_ref[...]) for r in halves]
    for r, xh in zip(halves, xs):
        gate = jax.nn.sigmoid(_dot(_rms(xh, gn_ref[...]).astype(BF16), wg_ref[...]))
        o_ref[r, :] = _rms(xh + gate * e[r], fn_ref[...])


def _ffn_ple(x, p, g, w1, w3, w2, gn, wg, wp, pn, fn, tm=512):
    t = x.shape[0]
    row = pl.BlockSpec((tm, D_MODEL), lambda i: (i, 0))
    vec = _const_spec((1, D_MODEL))
    return pl.pallas_call(
        _ffn_ple_kernel,
        grid=(t // tm,),
        in_specs=[row, pl.BlockSpec((tm, PLE_DIM), lambda i: (i, 0)), vec, _const_spec((D_MODEL, D_FF)),
                  _const_spec((D_MODEL, D_FF)), _const_spec((D_FF, D_MODEL)), vec,
                  _const_spec((D_MODEL, D_MODEL)), _const_spec((PLE_DIM, D_MODEL)), vec, vec],
        out_specs=row,
        out_shape=jax.ShapeDtypeStruct((t, D_MODEL), F32),
        compiler_params=_params(1),
        name="ffn_ple",
    )(x, p, g, w1, w3, w2, gn, wg, wp, pn, fn)


def kernel(x, p, ffn1_norm, ffn1_w1, ffn1_w3, ffn1_w2, mix_norm, w_in, hgrn_lb, hgrn_onorm, w_branch_a, pool_w, pool_scale, w_branch_b, w_out, ffn2_norm, ffn2_w1, ffn2_w3, ffn2_w2, ple_norm, ple_w_gate, ple_w_proj, ple_post_norm, final_norm):
    batch, seq, d = x.shape
    assert d == D_MODEL and p.shape[0] == 1 and hgrn_lb.shape == (2, HG_WIDTH)
    assert seq % PAIR == 0
    t = batch * seq
    bf = lambda w: w.astype(BF16)
    vec = lambda g: g.reshape(1, -1)

    xt = x.reshape(t, d)
    xt, w_in_b, wa, wo, wb, pw = _ffn(
        xt, vec(ffn1_norm[0]), bf(ffn1_w1[0]), bf(ffn1_w3[0]), bf(ffn1_w2[0]),
        cast=(w_in[0], w_branch_a[0], w_out[0], w_branch_b[0], pool_w[0].reshape(POOL_WIDTH, POOL_CH)))
    xt, w1, w3, w2, wg, wp = _mixer(
        xt, vec(mix_norm[0]), w_in_b, hgrn_lb, vec(hgrn_onorm[0]), wa, pw.reshape(pool_w.shape[1:]),
        vec(pool_scale[0]), wb, wo, seq, cast=(ffn2_w1[0], ffn2_w3[0], ffn2_w2[0], ple_w_gate[0], ple_w_proj[0]))
    out = _ffn_ple(xt, p[0].reshape(t, PLE_DIM), vec(ffn2_norm[0]), w1, w3, w2,
                   vec(ple_norm[0]), wg, wp, vec(ple_post_norm[0]), vec(final_norm))
    return out.reshape(batch, seq, d)
```

```python
import functools

import numpy as np
import jax
import jax.numpy as jnp
from jax import lax
from jax.experimental import pallas as pl
from jax.experimental.pallas import tpu as pltpu

D_MODEL = 1024
D_FF = 2816
PLE_DIM = 256
HEADS = 8
HEAD_DIM = 128
HG_WIDTH = HEADS * HEAD_DIM
POOL_WINDOWS = (2, 4, 8, 16)
POOL_CH = 128
POOL_WIDTH = len(POOL_WINDOWS) * POOL_CH
POOL_HALO = 16
IN_SPLITS = (0, 1024, 2048, 3072, 4096, 4608, 5632, 6656)
EPS = 1e-6
LOG2E = 1.4426950408889634

CHUNK = 128
FINE_HALVES = (2, 4)
LEVEL_HALVES = FINE_HALVES + (8, 16, 32, 64)
SLAB = 16
SUBLANES = 8
LANES = 128
DIAG_MASK = len(LEVEL_HALVES)
PAIR_MASK = DIAG_MASK + 1

TILE = 256
PAIR = 2 * TILE
PROJ_COLS = 256
MERGE_COLS = 256

V7X_VMEM_BYTES = 64 * 1024 * 1024
VMEM_LIMIT = V7X_VMEM_BYTES - 8 * 1024 * 1024

BF16 = jnp.bfloat16
F32 = jnp.float32


def _rms(x, g):
    return x * lax.rsqrt(jnp.mean(x * x, axis=-1, keepdims=True) + EPS) * g


def _silu(x):
    return x * jax.nn.sigmoid(x)


def _dot(a, b):
    return jnp.dot(a, b, preferred_element_type=F32)


def _dot_nt(a, b):
    return lax.dot_general(a, b, (((1,), (1,)), ((), ())), preferred_element_type=F32)


def _const_spec(shape):
    zeros = (0,) * len(shape)
    return pl.BlockSpec(shape, lambda *_: zeros, pipeline_mode=pl.Buffered(1))


def _params(n_grid_dims, sequential=False):
    sem = ("arbitrary" if sequential else "parallel",) * n_grid_dims
    return pltpu.CompilerParams(dimension_semantics=sem, vmem_limit_bytes=VMEM_LIMIT)


def _cast_specs(weights, n_steps):
    in_specs, out_specs, out_shapes = [], [], []
    for w in weights:
        rows, cols = w.shape
        slab = next(s for s in range(SLAB, rows + 1, SLAB) if rows % s == 0 and s * n_steps >= rows)
        last = rows // slab - 1
        index = lambda i, last=last: (jnp.minimum(i, last), 0)
        in_specs.append(pl.BlockSpec((slab, cols), index))
        out_specs.append(pl.BlockSpec((slab, cols), index))
        out_shapes.append(jax.ShapeDtypeStruct((rows, cols), BF16))
    return in_specs, out_specs, out_shapes


def _cast_slabs(src_refs, dst_refs):
    for src, dst in zip(src_refs, dst_refs):
        dst[...] = src[...].astype(BF16)


def _ffn_kernel(x_ref, g_ref, w1_ref, w3_ref, w2_ref, *rest):
    n_cast = len(rest) // 2
    o_ref = rest[n_cast]
    _cast_slabs(rest[:n_cast], rest[n_cast + 1:])
    x = x_ref[...]
    h = _rms(x, g_ref[...]).astype(BF16)
    a = _dot(h, w1_ref[...])
    b = _dot(h, w3_ref[...])
    act = (_silu(a) * b).astype(BF16)
    o_ref[...] = x + 0.5 * _dot(act, w2_ref[...])


def _ffn(x, g, w1, w3, w2, cast=(), tm=512):
    t = x.shape[0]
    row = pl.BlockSpec((tm, D_MODEL), lambda i: (i, 0))
    cast_in, cast_out, cast_shapes = _cast_specs(cast, t // tm)
    return pl.pallas_call(
        _ffn_kernel,
        grid=(t // tm,),
        in_specs=[row, _const_spec((1, D_MODEL)), _const_spec((D_MODEL, D_FF)),
                  _const_spec((D_MODEL, D_FF)), _const_spec((D_FF, D_MODEL))] + cast_in,
        out_specs=[row] + cast_out,
        out_shape=[jax.ShapeDtypeStruct((t, D_MODEL), F32)] + cast_shapes,
        compiler_params=_params(1),
        name="ffn",
    )(x, g, w1, w3, w2, *cast)


def _level_tables():
    c = CHUNK
    sums = np.zeros((len(FINE_HALVES) + 1, c, c), np.float32)
    masks = np.zeros((len(LEVEL_HALVES) + 2, c, c), np.float32)
    masks[DIAG_MASK] = np.eye(c)
    masks[PAIR_MASK, 1::2] = np.eye(c)[0::2]
    for li, b in enumerate(LEVEL_HALVES):
        for t in range(c):
            base = (t // (2 * b)) * 2 * b
            r = base + b - 1
            if t > r:
                masks[li, t, base:r + 1] = 1.0
            if b in FINE_HALVES:
                sums[li, t, (r + 1 if t > r else t + 1):(t + 1 if t > r else r + 1)] = 1.0
    for t in range(c):
        sums[len(FINE_HALVES), t, :t + 1] = 1.0
    return sums.reshape(-1, c), masks


PROJ_F32 = ("q", "k")
PROJ_BF16 = ("v", "lfh", "lfl", "og", "ga", "gb")
MERGE_NAMES = ("og", "ga", "gb", "u")


def _proj_pipe(x, g_ref, w_ref, lb_ref, dst):
    h = _rms(x, g_ref[...]).astype(BF16)
    lbp = lb_ref[...]
    e = jnp.exp(lbp - jnp.max(lbp, axis=0, keepdims=True))
    lb = e[0:1, :] / jnp.sum(e, axis=0, keepdims=True)
    yield

    def pieces(i):
        for lo in range(IN_SPLITS[i], IN_SPLITS[i + 1], PROJ_COLS):
            yield slice(lo - IN_SPLITS[i], lo - IN_SPLITS[i] + PROJ_COLS), _dot(h, w_ref[:, lo:lo + PROJ_COLS])

    for c, z in pieces(0):
        dst["q"][:, c] = _silu(z)
        yield
    for c, z in pieces(1):
        f = lb[:, c] + (1.0 - lb[:, c]) * jax.nn.sigmoid(z)
        lf = jnp.log(f) * LOG2E
        hi = lf.astype(BF16)
        dst["lfh"][:, c] = hi
        dst["lfl"][:, c] = (lf - hi.astype(F32)).astype(BF16)
        dst["k"][:, c] = 1.0 - f
        yield
    for c, z in pieces(2):
        dst["v"][:, c] = z.astype(BF16)
        yield
    for c, z in pieces(3):
        dst["og"][:, c] = _silu(z).astype(BF16)
        yield
    for c, z in pieces(4):
        dst["u"][:, c] = z.astype(BF16)
        yield
    for name, i in (("ga", 5), ("gb", 6)):
        for c, z in pieces(i):
            dst[name][:, c] = jax.nn.sigmoid(z).astype(BF16)
            yield


PROJ_PIECES = 1 + IN_SPLITS[-1] // PROJ_COLS


def _hgrn_pipe(src, o_dst, merge_dst, sums_ref, masks_ref, st_ref, expo_ref, new_sequence):
    for name in MERGE_NAMES:
        merge_dst[name][...] = src[name][...]
    yield

    row = lax.broadcasted_iota(jnp.int32, (CHUNK, 1), 0)
    head_cols = [slice(h * HEAD_DIM, (h + 1) * HEAD_DIM) for h in range(HEADS)]
    n_chunks = TILE // CHUNK

    def head_sums(x):
        return [jnp.sum(x[:, hc], axis=-1, keepdims=True) for hc in head_cols]

    def prev_row(x):
        return jnp.concatenate([pltpu.roll(x[r:r + SUBLANES], 1, 0) for r in range(0, CHUNK, SUBLANES)], axis=0)

    def intra(c):
        rows = slice(c * CHUNK, (c + 1) * CHUNK)
        g_at = len(FINE_HALVES) * CHUNK

        def q_rows(r0=0, n=CHUNK):
            return src["q"][c * CHUNK + r0:c * CHUNK + r0 + n, :]

        def k_rows(r0=0, n=CHUNK):
            return src["k"][c * CHUNK + r0:c * CHUNK + r0 + n, :]

        def g_rows(r0=0, n=CHUNK):
            return expo_ref[g_at + r0:g_at + r0 + n, :]

        expo_ref[...] = _dot(sums_ref[...], jnp.concatenate([src["lfh"][rows, :], src["lfl"][rows, :]], axis=0))

        k = k_rows()
        diag = head_sums(q_rows() * k)
        pair = head_sums(q_rows() * (1.0 - k) * prev_row(k))
        yield

        n_groups = CHUNK // SUBLANES
        groups = [[None] * n_groups for _ in range(HEADS)]

        def add(h, r0, val):
            for t in range(0, val.shape[0], SUBLANES):
                i = (r0 + t) // SUBLANES
                piece = val[t:t + SUBLANES]
                groups[h][i] = piece if groups[h][i] is None else groups[h][i] + piece

        for li, b in enumerate(LEVEL_HALVES):
            m = masks_ref[li]
            if b in FINE_HALVES:
                qk = jnp.where((row % (2 * b)) >= b, q_rows(), k_rows())
                x = (qk * jnp.exp2(expo_ref[li * CHUNK:(li + 1) * CHUNK, :])).astype(BF16)
                for h, hc in enumerate(head_cols):
                    add(h, 0, m * _dot_nt(x[:, hc], x[:, hc]))
            else:
                queries, keys = [], []
                for r0 in range(0, CHUNK, b):
                    r = (r0 // (2 * b)) * 2 * b + b - 1
                    if (r0 // b) % 2:
                        queries.append((r0, q_rows(r0, b) * jnp.exp2(g_rows(r0, b) - g_rows(r, 1))))
                    else:
                        keys.append((r0, k_rows(r0, b) * jnp.exp2(g_rows(r, 1) - g_rows(r0, b))))
                lhs = jnp.concatenate([p for _, p in queries], axis=0).astype(BF16)
                rhs = jnp.concatenate([p for _, p in keys], axis=0).astype(BF16)
                key_at = {r0: i * b for i, (r0, _) in enumerate(keys)}
                fill = jnp.zeros((lhs.shape[0], LANES - rhs.shape[0]), F32)
                for h, hc in enumerate(head_cols):
                    s = jnp.concatenate([_dot_nt(lhs[:, hc], rhs[:, hc]), fill], axis=1)
                    for i, (r0, _) in enumerate(queries):
                        blk = s[i * b:(i + 1) * b]
                        first_key = (r0 // (2 * b)) * 2 * b
                        if first_key != key_at[first_key]:
                            blk = pltpu.roll(blk, first_key - key_at[first_key], 1)
                        blk = blk[:, :CHUNK]
                        add(h, r0, blk if 2 * b == CHUNK else m[r0:r0 + b] * blk)
            yield
        q_state = (q_rows() * jnp.exp2(g_rows())).astype(BF16)
        k_end = k_rows() * jnp.exp2(g_rows(CHUNK - 1, 1) - g_rows())
        chunk_decay = jnp.exp2(g_rows(CHUNK - 1, 1))
        vb = src["v"][rows, :]
        o_intra, k_end_t, decay_col = [], [], []
        for h, hc in enumerate(head_cols):
            scores = jnp.concatenate(groups[h], axis=0)
            scores = scores + diag[h] * masks_ref[DIAG_MASK] + pair[h] * masks_ref[PAIR_MASK]
            o_intra.append(_dot(scores.astype(BF16), vb[:, hc]))
            k_end_t.append(k_end[:, hc].T.astype(BF16))
            decay_col.append(jnp.broadcast_to(chunk_decay[:, hc], (SUBLANES, HEAD_DIM)).T[:, 0:1])
        yield
        return o_intra, q_state, k_end_t, decay_col, vb

    def inter(c, o_intra, q_state, k_end_t, decay_col, vb):
        rows = slice(c * CHUNK, (c + 1) * CHUNK)
        states = [st_ref[h] for h in range(HEADS)]
        if c == 0:
            states = [jnp.where(new_sequence, 0.0, s) for s in states]
        for h, hc in enumerate(head_cols):
            o_dst[rows, hc] = (o_intra[h] + _dot(q_state[:, hc], states[h].astype(BF16))).astype(BF16)
        for h, hc in enumerate(head_cols):
            st_ref[h] = decay_col[h] * states[h] + _dot(k_end_t[h], vb[:, hc])

    pending = yield from intra(0)
    for c in range(n_chunks):
        nxt = (yield from intra(c + 1)) if c + 1 < n_chunks else None
        inter(c, *pending)
        yield
        pending = nxt


HGRN_PIECES = 1 + (TILE // CHUNK) * (len(LEVEL_HALVES) + 3)


def _merge_pipe(x_ref, rows, o_src, src, onorm_ref, wa_ref, pw_ref, ps_ref, wb_ref, wo_ref, out_ref,
                pool_ref, halo_ref, new_sequence, pos0):
    col_pieces = [slice(lo, lo + MERGE_COLS) for lo in range(0, D_MODEL, MERGE_COLS)]
    o = o_src[...].astype(F32)
    heads = []
    for h in range(HEADS):
        oh = o[:, h * HEAD_DIM:(h + 1) * HEAD_DIM]
        heads.append(oh * lax.rsqrt(jnp.mean(oh * oh, axis=-1, keepdims=True) + EPS))
        if h % 2:
            yield
    on = (jnp.concatenate(heads, axis=-1) * onorm_ref[...] * src["og"][...]).astype(BF16)
    ya = []
    for c in col_pieces:
        ya.append(_dot(on, wa_ref[:, c]))
        yield

    u = src["u"][...].astype(F32)
    ext = jnp.concatenate([jnp.where(new_sequence, 0.0, halo_ref[...]), u], axis=0)
    halo_ref[...] = u[TILE - POOL_HALO:TILE, :]
    body = slice(POOL_HALO, 2 * POOL_HALO + TILE)
    pos = (pos0 + 1 + lax.broadcasted_iota(jnp.int32, (TILE, POOL_CH), 0)).astype(F32)
    groups = []
    for g, w in enumerate(POOL_WINDOWS):
        cols = slice(g * POOL_CH, (g + 1) * POOL_CH)
        s = ext[:, cols]
        d = 1
        while d < w:
            pool_ref[body, cols] = s
            s = s + pool_ref[POOL_HALO - d:2 * POOL_HALO + TILE - d, cols]
            d *= 2
        tok = u[:, cols]
        pooled = s[POOL_HALO:] / jnp.minimum(pos, float(w)) - tok
        groups.append(_dot(pooled.astype(BF16), pw_ref[g]))
        yield
    mixed = (jnp.concatenate(groups, axis=-1) * ps_ref[...]).astype(BF16)
    y = []
    for i, c in enumerate(col_pieces):
        yb = _dot(mixed, wb_ref[:, c])
        y.append((src["ga"][:, c] * ya[i] + src["gb"][:, c] * yb).astype(BF16))
        yield
    y = jnp.concatenate(y, axis=-1)
    for c in col_pieces:
        out_ref[rows, c] = x_ref[rows, c] + _dot(y, wo_ref[:, c])
        yield


MERGE_PIECES = HEADS // 2 + len(POOL_WINDOWS) + 3 * (D_MODEL // MERGE_COLS)


def _interleave(pipes):
    done = [0] * len(pipes)
    alive = [True] * len(pipes)
    while any(alive):
        i = min((i for i in range(len(pipes)) if alive[i]), key=lambda i: (done[i] + 1) / pipes[i][1])
        try:
            next(pipes[i][0])
            done[i] += 1
        except StopIteration:
            alive[i] = False


def _mixer_kernel(xp_ref, xm_ref, g_ref, w_ref, lb_ref, sums_ref, masks_ref, onorm_ref, wa_ref, pw_ref, ps_ref,
                  wb_ref, wo_ref, *rest, tiles_per_seq, n_cast):
    out_ref = rest[n_cast]
    _cast_slabs(rest[:n_cast], rest[n_cast + 1:2 * n_cast + 1])
    scratch = list(rest[2 * n_cast + 1:])

    def take(names):
        return {n: scratch.pop(0) for n in names}

    proj_buf = [take(PROJ_F32 + PROJ_BF16 + ("u",)) for _ in range(2)]
    merge_buf = [take(MERGE_NAMES) for _ in range(2)]
    o_buf = [scratch.pop(0) for _ in range(2)]
    st_ref, pool_ref, halo_ref, expo_ref = scratch
    i = pl.program_id(0)

    @pl.when(i == 0)
    def _():
        for ref in list(proj_buf[1].values()) + list(merge_buf[0].values()) + [o_buf[0], st_ref, halo_ref, pool_ref]:
            ref[...] = jnp.zeros_like(ref)

    for a in range(2):
        g = 2 * i + a
        rows = slice(a * TILE, (a + 1) * TILE)
        seq_h = lax.rem(g - 1 + tiles_per_seq, tiles_per_seq)
        seq_m = lax.rem(g - 2 + tiles_per_seq, tiles_per_seq)
        _interleave([
            (_proj_pipe(xp_ref[rows, :], g_ref, w_ref, lb_ref, proj_buf[a]), PROJ_PIECES),
            (_hgrn_pipe(proj_buf[1 - a], o_buf[1 - a], merge_buf[1 - a], sums_ref, masks_ref, st_ref, expo_ref,
                        seq_h == 0), HGRN_PIECES),
            (_merge_pipe(xm_ref, rows, o_buf[a], merge_buf[a], onorm_ref, wa_ref, pw_ref, ps_ref, wb_ref, wo_ref,
                         out_ref, pool_ref, halo_ref, seq_m == 0, seq_m * TILE), MERGE_PIECES),
        ])


def _mixer(x, g, w_in, hgrn_lb, onorm, wa, pw, ps, wb, wo, seq, cast=()):
    t = x.shape[0]
    n_steps = t // PAIR + 1
    cast_in, cast_out, cast_shapes = _cast_specs(cast, n_steps)
    sums, masks = _level_tables()
    sums2 = jnp.asarray(np.concatenate([sums, sums], axis=1), BF16)
    tile = lambda width, dtype: pltpu.VMEM((TILE, width), dtype)
    proj_slot = ([tile(HG_WIDTH, F32)] * len(PROJ_F32) + [tile(HG_WIDTH, BF16)] * len(PROJ_BF16)
                 + [tile(POOL_WIDTH, BF16)])
    merge_slot = [tile(HG_WIDTH, BF16)] * 3 + [tile(POOL_WIDTH, BF16)]
    return pl.pallas_call(
        functools.partial(_mixer_kernel, tiles_per_seq=seq // TILE, n_cast=len(cast)),
        grid=(n_steps,),
        in_specs=[pl.BlockSpec((PAIR, D_MODEL), lambda i: (jnp.minimum(i, n_steps - 2), 0)),
                  pl.BlockSpec((PAIR, D_MODEL), lambda i: (jnp.maximum(i - 1, 0), 0)),
                  _const_spec((1, D_MODEL)), _const_spec((D_MODEL, IN_SPLITS[-1])), _const_spec(hgrn_lb.shape),
                  _const_spec(sums2.shape), _const_spec(masks.shape),
                  _const_spec((1, HG_WIDTH)), _const_spec((HG_WIDTH, D_MODEL)),
                  _const_spec((len(POOL_WINDOWS), POOL_CH, POOL_CH)), _const_spec((1, POOL_WIDTH)),
                  _const_spec((POOL_WIDTH, D_MODEL)), _const_spec((D_MODEL, D_MODEL))] + cast_in,
        out_specs=[pl.BlockSpec((PAIR, D_MODEL), lambda i: (jnp.maximum(i - 1, 0), 0))] + cast_out,
        out_shape=[jax.ShapeDtypeStruct((t, D_MODEL), F32)] + cast_shapes,
        scratch_shapes=(proj_slot * 2 + merge_slot * 2 + [tile(HG_WIDTH, BF16)] * 2
                        + [pltpu.VMEM((HEADS, HEAD_DIM, HEAD_DIM), F32),
                           pltpu.VMEM((2 * POOL_HALO + TILE, POOL_WIDTH), F32),
                           pltpu.VMEM((POOL_HALO, POOL_WIDTH), F32),
                           pltpu.VMEM(((len(FINE_HALVES) + 1) * CHUNK, HG_WIDTH), F32)]),
        compiler_params=_params(1, sequential=True),
        name="mixer",
    )(x, x, g, w_in, hgrn_lb, sums2, jnp.asarray(masks), onorm, wa, pw, ps, wb, wo, *cast)


def _ffn_ple_kernel(x_ref, p_ref, g_ref, w1_ref, w3_ref, w2_ref, gn_ref, wg_ref, wp_ref, pn_ref, fn_ref, o_ref):
    x = x_ref[...]
    e = _rms(_dot(p_ref[...].astype(BF16), wp_ref[...]), pn_ref[...])
    h = _rms(x, g_ref[...]).astype(BF16)
    act = (_silu(_dot(h, w1_ref[...])) * _dot(h, w3_ref[...])).astype(BF16)
    half = x.shape[0] // 2
    halves = (slice(0, half), slice(half, 2 * half))
    xs = [x[r] + 0.5 * _dot(act[r], w2_ref[...]) for r in halves]
    for r, xh in zip(halves, xs):
        gate = jax.nn.sigmoid(_dot(_rms(xh, gn_ref[...]).astype(BF16), wg_ref[...]))
        o_ref[r, :] = _rms(xh + gate * e[r], fn_ref[...])


def _ffn_ple(x, p, g, w1, w3, w2, gn, wg, wp, pn, fn, tm=512):
    t = x.shape[0]
    row = pl.BlockSpec((tm, D_MODEL), lambda i: (i, 0))
    vec = _const_spec((1, D_MODEL))
    return pl.pallas_call(
        _ffn_ple_kernel,
        grid=(t // tm,),
        in_specs=[row, pl.BlockSpec((tm, PLE_DIM), lambda i: (i, 0)), vec, _const_spec((D_MODEL, D_FF)),
                  _const_spec((D_MODEL, D_FF)), _const_spec((D_FF, D_MODEL)), vec,
                  _const_spec((D_MODEL, D_MODEL)), _const_spec((PLE_DIM, D_MODEL)), vec, vec],
        out_specs=row,
        out_shape=jax.ShapeDtypeStruct((t, D_MODEL), F32),
        compiler_params=_params(1),
        name="ffn_ple",
    )(x, p, g, w1, w3, w2, gn, wg, wp, pn, fn)


def kernel(x, p, ffn1_norm, ffn1_w1, ffn1_w3, ffn1_w2, mix_norm, w_in, hgrn_lb, hgrn_onorm, w_branch_a, pool_w, pool_scale, w_branch_b, w_out, ffn2_norm, ffn2_w1, ffn2_w3, ffn2_w2, ple_norm, ple_w_gate, ple_w_proj, ple_post_norm, final_norm):
    batch, seq, d = x.shape
    assert d == D_MODEL and p.shape[0] == 1 and hgrn_lb.shape == (2, HG_WIDTH)
    assert seq % PAIR == 0
    t = batch * seq
    bf = lambda w: w.astype(BF16)
    vec = lambda g: g.reshape(1, -1)

    xt = x.reshape(t, d)
    xt, w_in_b, wa, wo, wb, pw = _ffn(
        xt, vec(ffn1_norm[0]), bf(ffn1_w1[0]), bf(ffn1_w3[0]), bf(ffn1_w2[0]),
        cast=(w_in[0], w_branch_a[0], w_out[0], w_branch_b[0], pool_w[0].reshape(POOL_WIDTH, POOL_CH)))
    xt, w1, w3, w2, wg, wp = _mixer(
        xt, vec(mix_norm[0]), w_in_b, hgrn_lb, vec(hgrn_onorm[0]), wa, pw.reshape(pool_w.shape[1:]),
        vec(pool_scale[0]), wb, wo, seq, cast=(ffn2_w1[0], ffn2_w3[0], ffn2_w2[0], ple_w_gate[0], ple_w_proj[0]))
    out = _ffn_ple(xt, p[0].reshape(t, PLE_DIM), vec(ffn2_norm[0]), w1, w3, w2,
                   vec(ple_norm[0]), wg, wp, vec(ple_post_norm[0]), vec(final_norm))
    return out.reshape(batch, seq, d)
```

```python
import functools

import numpy as np
import jax
import jax.numpy as jnp
from jax import lax
from jax.experimental import pallas as pl
from jax.experimental.pallas import tpu as pltpu

D_MODEL = 1024
D_FF = 2816
PLE_DIM = 256
HEADS = 8
HEAD_DIM = 128
HG_WIDTH = HEADS * HEAD_DIM
POOL_WINDOWS = (2, 4, 8, 16)
POOL_CH = 128
POOL_WIDTH = len(POOL_WINDOWS) * POOL_CH
POOL_HALO = 16
IN_SPLITS = (0, 1024, 2048, 3072, 4096, 4608, 5632, 6656)
EPS = 1e-6
LOG2E = 1.4426950408889634

CHUNK = 128
FINE_HALVES = (2, 4)
LEVEL_HALVES = FINE_HALVES + (8, 16, 32, 64)
SLAB = 16
SUBLANES = 8
LANES = 128
DIAG_MASK = len(LEVEL_HALVES)
PAIR_MASK = DIAG_MASK + 1

TILE = 256
PAIR = 2 * TILE
PROJ_COLS = 256
MERGE_COLS = 256

V7X_VMEM_BYTES = 64 * 1024 * 1024
VMEM_LIMIT = V7X_VMEM_BYTES - 8 * 1024 * 1024

BF16 = jnp.bfloat16
F32 = jnp.float32


def _rms(x, g):
    return x * lax.rsqrt(jnp.mean(x * x, axis=-1, keepdims=True) + EPS) * g


def _silu(x):
    return x * jax.nn.sigmoid(x)


def _dot(a, b):
    return jnp.dot(a, b, preferred_element_type=F32)


def _dot_nt(a, b):
    return lax.dot_general(a, b, (((1,), (1,)), ((), ())), preferred_element_type=F32)


def _const_spec(shape):
    zeros = (0,) * len(shape)
    return pl.BlockSpec(shape, lambda *_: zeros, pipeline_mode=pl.Buffered(1))


def _params(n_grid_dims, sequential=False):
    sem = ("arbitrary" if sequential else "parallel",) * n_grid_dims
    return pltpu.CompilerParams(dimension_semantics=sem, vmem_limit_bytes=VMEM_LIMIT)


def _cast_specs(weights, n_steps):
    in_specs, out_specs, out_shapes = [], [], []
    for w in weights:
        rows, cols = w.shape
        slab = next(s for s in range(SLAB, rows + 1, SLAB) if rows % s == 0 and s * n_steps >= rows)
        last = rows // slab - 1
        index = lambda i, last=last: (jnp.minimum(i, last), 0)
        in_specs.append(pl.BlockSpec((slab, cols), index))
        out_specs.append(pl.BlockSpec((slab, cols), index))
        out_shapes.append(jax.ShapeDtypeStruct((rows, cols), BF16))
    return in_specs, out_specs, out_shapes


def _cast_slabs(src_refs, dst_refs):
    for src, dst in zip(src_refs, dst_refs):
        dst[...] = src[...].astype(BF16)


def _ffn_kernel(x_ref, g_ref, w1_ref, w3_ref, w2_ref, *rest):
    n_cast = len(rest) // 2
    o_ref = rest[n_cast]
    _cast_slabs(rest[:n_cast], rest[n_cast + 1:])
    x = x_ref[...]
    h = _rms(x, g_ref[...]).astype(BF16)
    a = _dot(h, w1_ref[...])
    b = _dot(h, w3_ref[...])
    act = (_silu(a) * b).astype(BF16)
    o_ref[...] = x + 0.5 * _dot(act, w2_ref[...])


def _ffn(x, g, w1, w3, w2, cast=(), tm=512):
    t = x.shape[0]
    row = pl.BlockSpec((tm, D_MODEL), lambda i: (i, 0))
    cast_in, cast_out, cast_shapes = _cast_specs(cast, t // tm)
    return pl.pallas_call(
        _ffn_kernel,
        grid=(t // tm,),
        in_specs=[row, _const_spec((1, D_MODEL)), _const_spec((D_MODEL, D_FF)),
                  _const_spec((D_MODEL, D_FF)), _const_spec((D_FF, D_MODEL))] + cast_in,
        out_specs=[row] + cast_out,
        out_shape=[jax.ShapeDtypeStruct((t, D_MODEL), F32)] + cast_shapes,
        compiler_params=_params(1),
        name="ffn",
    )(x, g, w1, w3, w2, *cast)


def _level_tables():
    c = CHUNK
    sums = np.zeros((len(FINE_HALVES) + 1, c, c), np.float32)
    masks = np.zeros((len(LEVEL_HALVES) + 2, c, c), np.float32)
    masks[DIAG_MASK] = np.eye(c)
    masks[PAIR_MASK, 1::2] = np.eye(c)[0::2]
    for li, b in enumerate(LEVEL_HALVES):
        for t in range(c):
            base = (t // (2 * b)) * 2 * b
            r = base + b - 1
            if t > r:
                masks[li, t, base:r + 1] = 1.0
            if b in FINE_HALVES:
                sums[li, t, (r + 1 if t > r else t + 1):(t + 1 if t > r else r + 1)] = 1.0
    for t in range(c):
        sums[len(FINE_HALVES), t, :t + 1] = 1.0
    return sums.reshape(-1, c), masks


PROJ_F32 = ("q", "k")
PROJ_BF16 = ("v", "lfh", "lfl", "og", "ga", "gb")
MERGE_NAMES = ("og", "ga", "gb", "u")


def _proj_pipe(x, g_ref, w_ref, lb_ref, dst):
    h = _rms(x, g_ref[...]).astype(BF16)
    lbp = lb_ref[...]
    e = jnp.exp(lbp - jnp.max(lbp, axis=0, keepdims=True))
    lb = e[0:1, :] / jnp.sum(e, axis=0, keepdims=True)
    yield

    def pieces(i):
        for lo in range(IN_SPLITS[i], IN_SPLITS[i + 1], PROJ_COLS):
            yield slice(lo - IN_SPLITS[i], lo - IN_SPLITS[i] + PROJ_COLS), _dot(h, w_ref[:, lo:lo + PROJ_COLS])

    for c, z in pieces(0):
        dst["q"][:, c] = _silu(z)
        yield
    for c, z in pieces(1):
        f = lb[:, c] + (1.0 - lb[:, c]) * jax.nn.sigmoid(z)
        lf = jnp.log(f) * LOG2E
        hi = lf.astype(BF16)
        dst["lfh"][:, c] = hi
        dst["lfl"][:, c] = (lf - hi.astype(F32)).astype(BF16)
        dst["k"][:, c] = 1.0 - f
        yield
    for c, z in pieces(2):
        dst["v"][:, c] = z.astype(BF16)
        yield
    for c, z in pieces(3):
        dst["og"][:, c] = _silu(z).astype(BF16)
        yield
    for c, z in pieces(4):
        dst["u"][:, c] = z.astype(BF16)
        yield
    for name, i in (("ga", 5), ("gb", 6)):
        for c, z in pieces(i):
            dst[name][:, c] = jax.nn.sigmoid(z).astype(BF16)
            yield


PROJ_PIECES = 1 + IN_SPLITS[-1] // PROJ_COLS


def _hgrn_pipe(src, o_dst, merge_dst, sums_ref, masks_ref, st_ref, new_sequence):
    for name in MERGE_NAMES:
        merge_dst[name][...] = src[name][...]
    yield

    row = lax.broadcasted_iota(jnp.int32, (CHUNK, 1), 0)
    head_cols = [slice(h * HEAD_DIM, (h + 1) * HEAD_DIM) for h in range(HEADS)]
    n_chunks = TILE // CHUNK

    def head_sums(x):
        return [jnp.sum(x[:, hc], axis=-1, keepdims=True) for hc in head_cols]

    def prev_row(x):
        return jnp.concatenate([pltpu.roll(x[r:r + SUBLANES], 1, 0) for r in range(0, CHUNK, SUBLANES)], axis=0)

    def intra(c):
        rows = slice(c * CHUNK, (c + 1) * CHUNK)
        q = src["q"][rows, :]
        k = src["k"][rows, :]
        vb = src["v"][rows, :]
        expo = _dot(sums_ref[...], jnp.concatenate([src["lfh"][rows, :], src["lfl"][rows, :]], axis=0))
        g = expo[len(FINE_HALVES) * CHUNK:, :]

        diag = head_sums(q * k)
        pair = head_sums(q * (1.0 - k) * prev_row(k))
        yield

        n_groups = CHUNK // SUBLANES
        groups = [[None] * n_groups for _ in range(HEADS)]

        def add(h, r0, val):
            for t in range(0, val.shape[0], SUBLANES):
                i = (r0 + t) // SUBLANES
                piece = val[t:t + SUBLANES]
                groups[h][i] = piece if groups[h][i] is None else groups[h][i] + piece

        for li, b in enumerate(LEVEL_HALVES):
            m = masks_ref[li]
            if b in FINE_HALVES:
                for h, hc in enumerate(head_cols):
                    qk = jnp.where((row % (2 * b)) >= b, q[:, hc], k[:, hc])
                    x = (qk * jnp.exp2(expo[li * CHUNK:(li + 1) * CHUNK, hc])).astype(BF16)
                    add(h, 0, m * _dot_nt(x, x))
            else:
                queries, keys = [], []
                for r0 in range(0, CHUNK, b):
                    r = (r0 // (2 * b)) * 2 * b + b - 1
                    if (r0 // b) % 2:
                        queries.append((r0, q[r0:r0 + b] * jnp.exp2(g[r0:r0 + b] - g[r:r + 1])))
                    else:
                        keys.append((r0, k[r0:r0 + b] * jnp.exp2(g[r:r + 1] - g[r0:r0 + b])))
                key_at = {r0: i * b for i, (r0, _) in enumerate(keys)}
                fill = jnp.zeros((len(queries) * b, LANES - len(keys) * b), F32)
                for h, hc in enumerate(head_cols):
                    lhs = jnp.concatenate([p[:, hc] for _, p in queries], axis=0).astype(BF16)
                    rhs = jnp.concatenate([p[:, hc] for _, p in keys], axis=0).astype(BF16)
                    s = jnp.concatenate([_dot_nt(lhs, rhs), fill], axis=1)
                    for i, (r0, _) in enumerate(queries):
                        blk = s[i * b:(i + 1) * b]
                        first_key = (r0 // (2 * b)) * 2 * b
                        if first_key != key_at[first_key]:
                            blk = pltpu.roll(blk, first_key - key_at[first_key], 1)
                        blk = blk[:, :CHUNK]
                        add(h, r0, blk if 2 * b == CHUNK else m[r0:r0 + b] * blk)
            yield
        q_state = (q * jnp.exp2(g)).astype(BF16)
        k_end = k * jnp.exp2(g[CHUNK - 1:CHUNK] - g)
        chunk_decay = jnp.exp2(g[CHUNK - 1:CHUNK])
        o_intra, k_end_t, decay_col = [], [], []
        for h, hc in enumerate(head_cols):
            scores = jnp.concatenate(groups[h], axis=0)
            scores = scores + diag[h] * masks_ref[DIAG_MASK] + pair[h] * masks_ref[PAIR_MASK]
            o_intra.append(_dot(scores.astype(BF16), vb[:, hc]))
            k_end_t.append(k_end[:, hc].T.astype(BF16))
            decay_col.append(jnp.broadcast_to(chunk_decay[:, hc], (SUBLANES, HEAD_DIM)).T[:, 0:1])
        yield
        return o_intra, q_state, k_end_t, decay_col, vb

    def inter(c, o_intra, q_state, k_end_t, decay_col, vb):
        rows = slice(c * CHUNK, (c + 1) * CHUNK)
        states = [st_ref[h] for h in range(HEADS)]
        if c == 0:
            states = [jnp.where(new_sequence, 0.0, s) for s in states]
        for h, hc in enumerate(head_cols):
            o_dst[rows, hc] = (o_intra[h] + _dot(q_state[:, hc], states[h].astype(BF16))).astype(BF16)
        for h, hc in enumerate(head_cols):
            st_ref[h] = decay_col[h] * states[h] + _dot(k_end_t[h], vb[:, hc])

    pending = yield from intra(0)
    for c in range(n_chunks):
        nxt = (yield from intra(c + 1)) if c + 1 < n_chunks else None
        inter(c, *pending)
        yield
        pending = nxt


HGRN_PIECES = 1 + (TILE // CHUNK) * (len(LEVEL_HALVES) + 3)


def _merge_pipe(x_ref, rows, o_src, src, onorm_ref, wa_ref, pw_ref, ps_ref, wb_ref, wo_ref, out_ref,
                pool_ref, halo_ref, new_sequence, pos0):
    col_pieces = [slice(lo, lo + MERGE_COLS) for lo in range(0, D_MODEL, MERGE_COLS)]
    o = o_src[...].astype(F32)
    heads = []
    for h in range(HEADS):
        oh = o[:, h * HEAD_DIM:(h + 1) * HEAD_DIM]
        heads.append(oh * lax.rsqrt(jnp.mean(oh * oh, axis=-1, keepdims=True) + EPS))
        if h % 2:
            yield
    on = (jnp.concatenate(heads, axis=-1) * onorm_ref[...] * src["og"][...]).astype(BF16)
    ya = []
    for c in col_pieces:
        ya.append(_dot(on, wa_ref[:, c]))
        yield

    u = src["u"][...].astype(F32)
    ext = jnp.concatenate([jnp.where(new_sequence, 0.0, halo_ref[...]), u], axis=0)
    halo_ref[...] = u[TILE - POOL_HALO:TILE, :]
    body = slice(POOL_HALO, 2 * POOL_HALO + TILE)
    pos = (pos0 + 1 + lax.broadcasted_iota(jnp.int32, (TILE, POOL_CH), 0)).astype(F32)
    groups = []
    for g, w in enumerate(POOL_WINDOWS):
        cols = slice(g * POOL_CH, (g + 1) * POOL_CH)
        s = ext[:, cols]
        d = 1
        while d < w:
            pool_ref[body, cols] = s
            s = s + pool_ref[POOL_HALO - d:2 * POOL_HALO + TILE - d, cols]
            d *= 2
        tok = u[:, cols]
        pooled = s[POOL_HALO:] / jnp.minimum(pos, float(w)) - tok
        groups.append(_dot(pooled.astype(BF16), pw_ref[g]))
        yield
    mixed = (jnp.concatenate(groups, axis=-1) * ps_ref[...]).astype(BF16)
    y = []
    for i, c in enumerate(col_pieces):
        yb = _dot(mixed, wb_ref[:, c])
        y.append((src["ga"][:, c] * ya[i] + src["gb"][:, c] * yb).astype(BF16))
        yield
    y = jnp.concatenate(y, axis=-1)
    for c in col_pieces:
        out_ref[rows, c] = x_ref[rows, c] + _dot(y, wo_ref[:, c])
        yield


MERGE_PIECES = HEADS // 2 + len(POOL_WINDOWS) + 3 * (D_MODEL // MERGE_COLS)


def _interleave(pipes):
    done = [0] * len(pipes)
    alive = [True] * len(pipes)
    while any(alive):
        i = min((i for i in range(len(pipes)) if alive[i]), key=lambda i: (done[i] + 1) / pipes[i][1])
        try:
            next(pipes[i][0])
            done[i] += 1
        except StopIteration:
            alive[i] = False


def _mixer_kernel(xp_ref, xm_ref, g_ref, w_ref, lb_ref, sums_ref, masks_ref, onorm_ref, wa_ref, pw_ref, ps_ref,
                  wb_ref, wo_ref, *rest, tiles_per_seq, n_cast):
    out_ref = rest[n_cast]
    _cast_slabs(rest[:n_cast], rest[n_cast + 1:2 * n_cast + 1])
    scratch = list(rest[2 * n_cast + 1:])

    def take(names):
        return {n: scratch.pop(0) for n in names}

    proj_buf = [take(PROJ_F32 + PROJ_BF16 + ("u",)) for _ in range(2)]
    merge_buf = [take(MERGE_NAMES) for _ in range(2)]
    o_buf = [scratch.pop(0) for _ in range(2)]
    st_ref, pool_ref, halo_ref = scratch
    i = pl.program_id(0)

    @pl.when(i == 0)
    def _():
        for ref in list(proj_buf[1].values()) + list(merge_buf[0].values()) + [o_buf[0], st_ref, halo_ref, pool_ref]:
            ref[...] = jnp.zeros_like(ref)

    for a in range(2):
        g = 2 * i + a
        rows = slice(a * TILE, (a + 1) * TILE)
        seq_h = lax.rem(g - 1 + tiles_per_seq, tiles_per_seq)
        seq_m = lax.rem(g - 2 + tiles_per_seq, tiles_per_seq)
        _interleave([
            (_proj_pipe(xp_ref[rows, :], g_ref, w_ref, lb_ref, proj_buf[a]), PROJ_PIECES),
            (_hgrn_pipe(proj_buf[1 - a], o_buf[1 - a], merge_buf[1 - a], sums_ref, masks_ref, st_ref,
                        seq_h == 0), HGRN_PIECES),
            (_merge_pipe(xm_ref, rows, o_buf[a], merge_buf[a], onorm_ref, wa_ref, pw_ref, ps_ref, wb_ref, wo_ref,
                         out_ref, pool_ref, halo_ref, seq_m == 0, seq_m * TILE), MERGE_PIECES),
        ])


def _mixer(x, g, w_in, hgrn_lb, onorm, wa, pw, ps, wb, wo, seq, cast=()):
    t = x.shape[0]
    n_steps = t // PAIR + 1
    cast_in, cast_out, cast_shapes = _cast_specs(cast, n_steps)
    sums, masks = _level_tables()
    sums2 = jnp.asarray(np.concatenate([sums, sums], axis=1), BF16)
    tile = lambda width, dtype: pltpu.VMEM((TILE, width), dtype)
    proj_slot = ([tile(HG_WIDTH, F32)] * len(PROJ_F32) + [tile(HG_WIDTH, BF16)] * len(PROJ_BF16)
                 + [tile(POOL_WIDTH, BF16)])
    merge_slot = [tile(HG_WIDTH, BF16)] * 3 + [tile(POOL_WIDTH, BF16)]
    return pl.pallas_call(
        functools.partial(_mixer_kernel, tiles_per_seq=seq // TILE, n_cast=len(cast)),
        grid=(n_steps,),
        in_specs=[pl.BlockSpec((PAIR, D_MODEL), lambda i: (jnp.minimum(i, n_steps - 2), 0)),
                  pl.BlockSpec((PAIR, D_MODEL), lambda i: (jnp.maximum(i - 1, 0), 0)),
                  _const_spec((1, D_MODEL)), _const_spec((D_MODEL, IN_SPLITS[-1])), _const_spec(hgrn_lb.shape),
                  _const_spec(sums2.shape), _const_spec(masks.shape),
                  _const_spec((1, HG_WIDTH)), _const_spec((HG_WIDTH, D_MODEL)),
                  _const_spec((len(POOL_WINDOWS), POOL_CH, POOL_CH)), _const_spec((1, POOL_WIDTH)),
                  _const_spec((POOL_WIDTH, D_MODEL)), _const_spec((D_MODEL, D_MODEL))] + cast_in,
        out_specs=[pl.BlockSpec((PAIR, D_MODEL), lambda i: (jnp.maximum(i - 1, 0), 0))] + cast_out,
        out_shape=[jax.ShapeDtypeStruct((t, D_MODEL), F32)] + cast_shapes,
        scratch_shapes=(proj_slot * 2 + merge_slot * 2 + [tile(HG_WIDTH, BF16)] * 2
                        + [pltpu.VMEM((HEADS, HEAD_DIM, HEAD_DIM), F32),
                           pltpu.VMEM((2 * POOL_HALO + TILE, POOL_WIDTH), F32),
                           pltpu.VMEM((POOL_HALO, POOL_WIDTH), F32)]),
        compiler_params=_params(1, sequential=True),
        name="mixer",
    )(x, x, g, w_in, hgrn_lb, sums2, jnp.asarray(masks), onorm, wa, pw, ps, wb, wo, *cast)


def _ffn_ple_kernel(x_ref, p_ref, g_ref, w1_ref, w3_ref, w2_ref, gn_ref, wg_ref, wp_ref, pn_ref, fn_ref, o_ref):
    x = x_ref[...]
    e = _rms(_dot(p_ref[...].astype(BF16), wp_ref[...]), pn_ref[...])
    h = _rms(x, g_ref[...]).astype(BF16)
    act = (_silu(_dot(h, w1_ref[...])) * _dot(h, w3_ref[...])).astype(BF16)
    half = x.shape[0] // 2
    halves = (slice(0, half), slice(half, 2 * half))
    xs = [x[r] + 0.5 * _dot(act[r], w2_ref[...]) for r in halves]
    for r, xh in zip(halves, xs):
        gate = jax.nn.sigmoid(_dot(_rms(xh, gn_ref[...]).astype(BF16), wg_ref[...]))
        o_ref[r, :] = _rms(xh + gate * e[r], fn_ref[...])


def _ffn_ple(x, p, g, w1, w3, w2, gn, wg, wp, pn, fn, tm=512):
    t = x.shape[0]
    row = pl.BlockSpec((tm, D_MODEL), lambda i: (i, 0))
    vec = _const_spec((1, D_MODEL))
    return pl.pallas_call(
        _ffn_ple_kernel,
        grid=(t // tm,),
        in_specs=[row, pl.BlockSpec((tm, PLE_DIM), lambda i: (i, 0)), vec, _const_spec((D_MODEL, D_FF)),
                  _const_spec((D_MODEL, D_FF)), _const_spec((D_FF, D_MODEL)), vec,
                  _const_spec((D_MODEL, D_MODEL)), _const_spec((PLE_DIM, D_MODEL)), vec, vec],
        out_specs=row,
        out_shape=jax.ShapeDtypeStruct((t, D_MODEL), F32),
        compiler_params=_params(1),
        name="ffn_ple",
    )(x, p, g, w1, w3, w2, gn, wg, wp, pn, fn)


def kernel(x, p, ffn1_norm, ffn1_w1, ffn1_w3, ffn1_w2, mix_norm, w_in, hgrn_lb, hgrn_onorm, w_branch_a, pool_w, pool_scale, w_branch_b, w_out, ffn2_norm, ffn2_w1, ffn2_w3, ffn2_w2, ple_norm, ple_w_gate, ple_w_proj, ple_post_norm, final_norm):
    batch, seq, d = x.shape
    assert d == D_MODEL and p.shape[0] == 1 and hgrn_lb.shape == (2, HG_WIDTH)
    assert seq % PAIR == 0
    t = batch * seq
    bf = lambda w: w.astype(BF16)
    vec = lambda g: g.reshape(1, -1)

    xt = x.reshape(t, d)
    xt, w_in_b, wa, wo, wb, pw = _ffn(
        xt, vec(ffn1_norm[0]), bf(ffn1_w1[0]), bf(ffn1_w3[0]), bf(ffn1_w2[0]),
        cast=(w_in[0], w_branch_a[0], w_out[0], w_branch_b[0], pool_w[0].reshape(POOL_WIDTH, POOL_CH)))
    xt, w1, w3, w2, wg, wp = _mixer(
        xt, vec(mix_norm[0]), w_in_b, hgrn_lb, vec(hgrn_onorm[0]), wa, pw.reshape(pool_w.shape[1:]),
        vec(pool_scale[0]), wb, wo, seq, cast=(ffn2_w1[0], ffn2_w3[0], ffn2_w2[0], ple_w_gate[0], ple_w_proj[0]))
    out = _ffn_ple(xt, p[0].reshape(t, PLE_DIM), vec(ffn2_norm[0]), w1, w3, w2,
                   vec(ple_norm[0]), wg, wp, vec(ple_post_norm[0]), vec(final_norm))
    return out.reshape(batch, seq, d)
```

```python
import functools

import numpy as np
import jax
import jax.numpy as jnp
from jax import lax
from jax.experimental import pallas as pl
from jax.experimental.pallas import tpu as pltpu

D_MODEL = 1024
D_FF = 2816
PLE_DIM = 256
HEADS = 8
HEAD_DIM = 128
HG_WIDTH = HEADS * HEAD_DIM
POOL_WINDOWS = (2, 4, 8, 16)
POOL_CH = 128
POOL_WIDTH = len(POOL_WINDOWS) * POOL_CH
POOL_HALO = 16
IN_SPLITS = (0, 1024, 2048, 3072, 4096, 4608, 5632, 6656)
EPS = 1e-6

CHUNK = 128
FINE_HALVES = (2, 4)
LEVEL_HALVES = FINE_HALVES + (8, 16, 32, 64)
SLAB = 16
SUBLANES = 8
LANES = 128
DIAG_MASK = len(LEVEL_HALVES)
PAIR_MASK = DIAG_MASK + 1

TILE = 256
PAIR = 2 * TILE
PROJ_COLS = 256
MERGE_COLS = 256

V7X_VMEM_BYTES = 64 * 1024 * 1024
VMEM_LIMIT = V7X_VMEM_BYTES - 8 * 1024 * 1024

BF16 = jnp.bfloat16
F32 = jnp.float32


def _rms(x, g):
    return x * lax.rsqrt(jnp.mean(x * x, axis=-1, keepdims=True) + EPS) * g


def _sigmoid(x):
    return 0.5 * jnp.tanh(0.5 * x) + 0.5


def _silu(x):
    h = 0.5 * x
    return h * jnp.tanh(h) + h


def _dot(a, b):
    return jnp.dot(a, b, preferred_element_type=F32)


def _dot_nt(a, b):
    return lax.dot_general(a, b, (((1,), (1,)), ((), ())), preferred_element_type=F32)


def _const_spec(shape):
    zeros = (0,) * len(shape)
    return pl.BlockSpec(shape, lambda *_: zeros, pipeline_mode=pl.Buffered(1))


def _params(n_grid_dims, sequential=False):
    sem = ("arbitrary" if sequential else "parallel",) * n_grid_dims
    return pltpu.CompilerParams(dimension_semantics=sem, vmem_limit_bytes=VMEM_LIMIT)


def _cast_specs(weights, n_steps):
    in_specs, out_specs, out_shapes = [], [], []
    for w in weights:
        rows, cols = w.shape
        slab = next(s for s in range(SLAB, rows + 1, SLAB) if rows % s == 0 and s * n_steps >= rows)
        last = rows // slab - 1
        index = lambda i, last=last: (jnp.minimum(i, last), 0)
        in_specs.append(pl.BlockSpec((slab, cols), index))
        out_specs.append(pl.BlockSpec((slab, cols), index))
        out_shapes.append(jax.ShapeDtypeStruct((rows, cols), BF16))
    return in_specs, out_specs, out_shapes


def _cast_slabs(src_refs, dst_refs):
    for src, dst in zip(src_refs, dst_refs):
        dst[...] = src[...].astype(BF16)


def _ffn_kernel(x_ref, g_ref, w1_ref, w3_ref, w2_ref, *rest):
    n_cast = len(rest) // 2
    o_ref = rest[n_cast]
    _cast_slabs(rest[:n_cast], rest[n_cast + 1:])
    x = x_ref[...]
    h = _rms(x, g_ref[...]).astype(BF16)
    a = _dot(h, w1_ref[...])
    b = _dot(h, w3_ref[...])
    act = (_silu(a) * b).astype(BF16)
    o_ref[...] = x + 0.5 * _dot(act, w2_ref[...])


def _ffn(x, g, w1, w3, w2, cast=(), tm=512):
    t = x.shape[0]
    row = pl.BlockSpec((tm, D_MODEL), lambda i: (i, 0))
    cast_in, cast_out, cast_shapes = _cast_specs(cast, t // tm)
    return pl.pallas_call(
        _ffn_kernel,
        grid=(t // tm,),
        in_specs=[row, _const_spec((1, D_MODEL)), _const_spec((D_MODEL, D_FF)),
                  _const_spec((D_MODEL, D_FF)), _const_spec((D_FF, D_MODEL))] + cast_in,
        out_specs=[row] + cast_out,
        out_shape=[jax.ShapeDtypeStruct((t, D_MODEL), F32)] + cast_shapes,
        compiler_params=_params(1),
        name="ffn",
    )(x, g, w1, w3, w2, *cast)


def _level_tables():
    c = CHUNK
    sums = np.zeros((len(FINE_HALVES) + 1, c, c), np.float32)
    masks = np.zeros((len(LEVEL_HALVES) + 2, c, c), np.float32)
    masks[DIAG_MASK] = np.eye(c)
    masks[PAIR_MASK, 1::2] = np.eye(c)[0::2]
    for li, b in enumerate(LEVEL_HALVES):
        for t in range(c):
            base = (t // (2 * b)) * 2 * b
            r = base + b - 1
            if t > r:
                masks[li, t, base:r + 1] = 1.0
            if b in FINE_HALVES:
                sums[li, t, (r + 1 if t > r else t + 1):(t + 1 if t > r else r + 1)] = 1.0
    for t in range(c):
        sums[len(FINE_HALVES), t, :t + 1] = 1.0
    return sums.reshape(-1, c), masks


PROJ_F32 = ("q", "k")
PROJ_BF16 = ("v", "lfh", "lfl", "og", "ga", "gb")
MERGE_NAMES = ("og", "ga", "gb", "u")


def _proj_pipe(x, g_ref, w_ref, lb_ref, dst):
    h = _rms(x, g_ref[...]).astype(BF16)
    lbp = lb_ref[...]
    e = jnp.exp(lbp - jnp.max(lbp, axis=0, keepdims=True))
    lb = e[0:1, :] / jnp.sum(e, axis=0, keepdims=True)
    yield

    def pieces(i):
        for lo in range(IN_SPLITS[i], IN_SPLITS[i + 1], PROJ_COLS):
            yield slice(lo - IN_SPLITS[i], lo - IN_SPLITS[i] + PROJ_COLS), _dot(h, w_ref[:, lo:lo + PROJ_COLS])

    for c, z in pieces(0):
        dst["q"][:, c] = _silu(z)
        yield
    for c, z in pieces(1):
        f = lb[:, c] + (1.0 - lb[:, c]) * _sigmoid(z)
        lf = jnp.log2(f)
        hi = lf.astype(BF16)
        dst["lfh"][:, c] = hi
        dst["lfl"][:, c] = (lf - hi.astype(F32)).astype(BF16)
        dst["k"][:, c] = 1.0 - f
        yield
    for c, z in pieces(2):
        dst["v"][:, c] = z.astype(BF16)
        yield
    for c, z in pieces(3):
        dst["og"][:, c] = _silu(z).astype(BF16)
        yield
    for c, z in pieces(4):
        dst["u"][:, c] = z.astype(BF16)
        yield
    for name, i in (("ga", 5), ("gb", 6)):
        for c, z in pieces(i):
            dst[name][:, c] = _sigmoid(z).astype(BF16)
            yield


PROJ_PIECES = 1 + IN_SPLITS[-1] // PROJ_COLS


def _hgrn_pipe(src, o_dst, merge_dst, sums_ref, masks_ref, st_ref, new_sequence):
    for name in MERGE_NAMES:
        merge_dst[name][...] = src[name][...]
    yield

    row = lax.broadcasted_iota(jnp.int32, (CHUNK, 1), 0)
    head_cols = [slice(h * HEAD_DIM, (h + 1) * HEAD_DIM) for h in range(HEADS)]
    n_chunks = TILE // CHUNK

    def head_sums(x):
        return [jnp.sum(x[:, hc], axis=-1, keepdims=True) for hc in head_cols]

    def prev_row(x):
        return jnp.concatenate([pltpu.roll(x[r:r + SUBLANES], 1, 0) for r in range(0, CHUNK, SUBLANES)], axis=0)

    def intra(c):
        rows = slice(c * CHUNK, (c + 1) * CHUNK)
        q = src["q"][rows, :]
        k = src["k"][rows, :]
        vb = src["v"][rows, :]
        expo = _dot(sums_ref[...], jnp.concatenate([src["lfh"][rows, :], src["lfl"][rows, :]], axis=0))
        g = expo[len(FINE_HALVES) * CHUNK:, :]

        diag = head_sums(q * k)
        pair = head_sums(q * (1.0 - k) * prev_row(k))
        yield

        n_groups = CHUNK // SUBLANES
        groups = [[None] * n_groups for _ in range(HEADS)]

        def add(h, r0, val):
            for t in range(0, val.shape[0], SUBLANES):
                i = (r0 + t) // SUBLANES
                piece = val[t:t + SUBLANES]
                groups[h][i] = piece if groups[h][i] is None else groups[h][i] + piece

        for li, b in enumerate(LEVEL_HALVES):
            m = masks_ref[li]
            if b in FINE_HALVES:
                qk = jnp.where((row % (2 * b)) >= b, q, k)
                x = (qk * jnp.exp2(expo[li * CHUNK:(li + 1) * CHUNK, :])).astype(BF16)
                for h, hc in enumerate(head_cols):
                    add(h, 0, m * _dot_nt(x[:, hc], x[:, hc]))
            else:
                queries, keys = [], []
                for r0 in range(0, CHUNK, b):
                    r = (r0 // (2 * b)) * 2 * b + b - 1
                    if (r0 // b) % 2:
                        queries.append((r0, q[r0:r0 + b] * jnp.exp2(g[r0:r0 + b] - g[r:r + 1])))
                    else:
                        keys.append((r0, k[r0:r0 + b] * jnp.exp2(g[r:r + 1] - g[r0:r0 + b])))
                lhs = jnp.concatenate([p for _, p in queries], axis=0).astype(BF16)
                rhs = jnp.concatenate([p for _, p in keys], axis=0).astype(BF16)
                key_at = {r0: i * b for i, (r0, _) in enumerate(keys)}
                fill = jnp.zeros((lhs.shape[0], LANES - rhs.shape[0]), F32)
                for h, hc in enumerate(head_cols):
                    s = jnp.concatenate([_dot_nt(lhs[:, hc], rhs[:, hc]), fill], axis=1)
                    for i, (r0, _) in enumerate(queries):
                        blk = s[i * b:(i + 1) * b]
                        first_key = (r0 // (2 * b)) * 2 * b
                        if first_key != key_at[first_key]:
                            blk = pltpu.roll(blk, first_key - key_at[first_key], 1)
                        blk = blk[:, :CHUNK]
                        add(h, r0, blk if 2 * b == CHUNK else m[r0:r0 + b] * blk)
            yield
        q_state = (q * jnp.exp2(g)).astype(BF16)
        k_end = k * jnp.exp2(g[CHUNK - 1:CHUNK] - g)
        chunk_decay = jnp.exp2(g[CHUNK - 1:CHUNK])
        o_intra, kv, decay_col = [], [], []
        for h, hc in enumerate(head_cols):
            scores = jnp.concatenate(groups[h], axis=0)
            scores = scores + diag[h] * masks_ref[DIAG_MASK] + pair[h] * masks_ref[PAIR_MASK]
            k_end_t = k_end[:, hc].T.astype(BF16)
            both = _dot(jnp.concatenate([scores.astype(BF16), k_end_t], axis=0), vb[:, hc])
            o_intra.append(both[:CHUNK])
            kv.append(both[CHUNK:])
            decay_col.append(jnp.broadcast_to(chunk_decay[:, hc], (SUBLANES, HEAD_DIM)).T[:, 0:1])
        yield
        return o_intra, q_state, kv, decay_col

    def inter(c, o_intra, q_state, kv, decay_col):
        rows = slice(c * CHUNK, (c + 1) * CHUNK)
        states = [st_ref[h] for h in range(HEADS)]
        if c == 0:
            states = [jnp.where(new_sequence, 0.0, s) for s in states]
        for h, hc in enumerate(head_cols):
            o_dst[rows, hc] = (o_intra[h] + _dot(q_state[:, hc], states[h].astype(BF16))).astype(BF16)
        for h, hc in enumerate(head_cols):
            st_ref[h] = decay_col[h] * states[h] + kv[h]

    pending = yield from intra(0)
    for c in range(n_chunks):
        nxt = (yield from intra(c + 1)) if c + 1 < n_chunks else None
        inter(c, *pending)
        yield
        pending = nxt


HGRN_PIECES = 1 + (TILE // CHUNK) * (len(LEVEL_HALVES) + 3)


def _merge_pipe(x_ref, rows, o_src, src, onorm_ref, wa_ref, pw_ref, ps_ref, wb_ref, wo_ref, out_ref,
                pool_ref, halo_ref, new_sequence, pos0):
    col_pieces = [slice(lo, lo + MERGE_COLS) for lo in range(0, D_MODEL, MERGE_COLS)]
    o = o_src[...].astype(F32)
    heads = []
    for h in range(HEADS):
        oh = o[:, h * HEAD_DIM:(h + 1) * HEAD_DIM]
        heads.append(oh * lax.rsqrt(jnp.mean(oh * oh, axis=-1, keepdims=True) + EPS))
        if h % 2:
            yield
    on = (jnp.concatenate(heads, axis=-1) * onorm_ref[...] * src["og"][...]).astype(BF16)
    ya = []
    for c in col_pieces:
        ya.append(_dot(on, wa_ref[:, c]))
        yield

    u = src["u"][...].astype(F32)
    ext = jnp.concatenate([jnp.where(new_sequence, 0.0, halo_ref[...]), u], axis=0)
    halo_ref[...] = u[TILE - POOL_HALO:TILE, :]
    body = slice(POOL_HALO, 2 * POOL_HALO + TILE)
    pos = (pos0 + 1 + lax.broadcasted_iota(jnp.int32, (TILE, POOL_CH), 0)).astype(F32)
    groups = []
    for g, w in enumerate(POOL_WINDOWS):
        cols = slice(g * POOL_CH, (g + 1) * POOL_CH)
        s = ext[:, cols]
        d = 1
        while d < w:
            pool_ref[body, cols] = s
            s = s + pool_ref[POOL_HALO - d:2 * POOL_HALO + TILE - d, cols]
            d *= 2
        tok = u[:, cols]
        pooled = s[POOL_HALO:] / jnp.minimum(pos, float(w)) - tok
        groups.append(_dot(pooled.astype(BF16), pw_ref[g]))
        yield
    mixed = (jnp.concatenate(groups, axis=-1) * ps_ref[...]).astype(BF16)
    y = []
    for i, c in enumerate(col_pieces):
        yb = _dot(mixed, wb_ref[:, c])
        y.append((src["ga"][:, c] * ya[i] + src["gb"][:, c] * yb).astype(BF16))
        yield
    y = jnp.concatenate(y, axis=-1)
    for c in col_pieces:
        out_ref[rows, c] = x_ref[rows, c] + _dot(y, wo_ref[:, c])
        yield


MERGE_PIECES = HEADS // 2 + len(POOL_WINDOWS) + 3 * (D_MODEL // MERGE_COLS)


def _interleave(pipes):
    done = [0] * len(pipes)
    alive = [True] * len(pipes)
    while any(alive):
        i = min((i for i in range(len(pipes)) if alive[i]), key=lambda i: (done[i] + 1) / pipes[i][1])
        try:
            next(pipes[i][0])
            done[i] += 1
        except StopIteration:
            alive[i] = False


def _mixer_kernel(xp_ref, xm_ref, g_ref, w_ref, lb_ref, sums_ref, masks_ref, onorm_ref, wa_ref, pw_ref, ps_ref,
                  wb_ref, wo_ref, *rest, tiles_per_seq, n_cast):
    out_ref = rest[n_cast]
    _cast_slabs(rest[:n_cast], rest[n_cast + 1:2 * n_cast + 1])
    scratch = list(rest[2 * n_cast + 1:])

    def take(names):
        return {n: scratch.pop(0) for n in names}

    proj_buf = [take(PROJ_F32 + PROJ_BF16 + ("u",)) for _ in range(2)]
    merge_buf = [take(MERGE_NAMES) for _ in range(2)]
    o_buf = [scratch.pop(0) for _ in range(2)]
    st_ref, pool_ref, halo_ref = scratch
    i = pl.program_id(0)

    @pl.when(i == 0)
    def _():
        for ref in list(proj_buf[1].values()) + list(merge_buf[0].values()) + [o_buf[0], st_ref, halo_ref, pool_ref]:
            ref[...] = jnp.zeros_like(ref)

    for a in range(2):
        g = 2 * i + a
        rows = slice(a * TILE, (a + 1) * TILE)
        seq_h = lax.rem(g - 1 + tiles_per_seq, tiles_per_seq)
        seq_m = lax.rem(g - 2 + tiles_per_seq, tiles_per_seq)
        _interleave([
            (_proj_pipe(xp_ref[rows, :], g_ref, w_ref, lb_ref, proj_buf[a]), PROJ_PIECES),
            (_hgrn_pipe(proj_buf[1 - a], o_buf[1 - a], merge_buf[1 - a], sums_ref, masks_ref, st_ref,
                        seq_h == 0), HGRN_PIECES),
            (_merge_pipe(xm_ref, rows, o_buf[a], merge_buf[a], onorm_ref, wa_ref, pw_ref, ps_ref, wb_ref, wo_ref,
                         out_ref, pool_ref, halo_ref, seq_m == 0, seq_m * TILE), MERGE_PIECES),
        ])


def _mixer(x, g, w_in, hgrn_lb, onorm, wa, pw, ps, wb, wo, seq, cast=()):
    t = x.shape[0]
    n_steps = t // PAIR + 1
    cast_in, cast_out, cast_shapes = _cast_specs(cast, n_steps)
    sums, masks = _level_tables()
    sums2 = jnp.asarray(np.concatenate([sums, sums], axis=1), BF16)
    tile = lambda width, dtype: pltpu.VMEM((TILE, width), dtype)
    proj_slot = ([tile(HG_WIDTH, F32)] * len(PROJ_F32) + [tile(HG_WIDTH, BF16)] * len(PROJ_BF16)
                 + [tile(POOL_WIDTH, BF16)])
    merge_slot = [tile(HG_WIDTH, BF16)] * 3 + [tile(POOL_WIDTH, BF16)]
    return pl.pallas_call(
        functools.partial(_mixer_kernel, tiles_per_seq=seq // TILE, n_cast=len(cast)),
        grid=(n_steps,),
        in_specs=[pl.BlockSpec((PAIR, D_MODEL), lambda i: (jnp.minimum(i, n_steps - 2), 0)),
                  pl.BlockSpec((PAIR, D_MODEL), lambda i: (jnp.maximum(i - 1, 0), 0)),
                  _const_spec((1, D_MODEL)), _const_spec((D_MODEL, IN_SPLITS[-1])), _const_spec(hgrn_lb.shape),
                  _const_spec(sums2.shape), _const_spec(masks.shape),
                  _const_spec((1, HG_WIDTH)), _const_spec((HG_WIDTH, D_MODEL)),
                  _const_spec((len(POOL_WINDOWS), POOL_CH, POOL_CH)), _const_spec((1, POOL_WIDTH)),
                  _const_spec((POOL_WIDTH, D_MODEL)), _const_spec((D_MODEL, D_MODEL))] + cast_in,
        out_specs=[pl.BlockSpec((PAIR, D_MODEL), lambda i: (jnp.maximum(i - 1, 0), 0))] + cast_out,
        out_shape=[jax.ShapeDtypeStruct((t, D_MODEL), F32)] + cast_shapes,
        scratch_shapes=(proj_slot * 2 + merge_slot * 2 + [tile(HG_WIDTH, BF16)] * 2
                        + [pltpu.VMEM((HEADS, HEAD_DIM, HEAD_DIM), F32),
                           pltpu.VMEM((2 * POOL_HALO + TILE, POOL_WIDTH), F32),
                           pltpu.VMEM((POOL_HALO, POOL_WIDTH), F32)]),
        compiler_params=_params(1, sequential=True),
        name="mixer",
    )(x, x, g, w_in, hgrn_lb, sums2, jnp.asarray(masks), onorm, wa, pw, ps, wb, wo, *cast)


def _ffn_ple_kernel(x_ref, p_ref, g_ref, w1_ref, w3_ref, w2_ref, gn_ref, wg_ref, wp_ref, pn_ref, fn_ref, o_ref):
    x = x_ref[...]
    e = _rms(_dot(p_ref[...].astype(BF16), wp_ref[...]), pn_ref[...])
    h = _rms(x, g_ref[...]).astype(BF16)
    act = (_silu(_dot(h, w1_ref[...])) * _dot(h, w3_ref[...])).astype(BF16)
    half = x.shape[0] // 2
    halves = (slice(0, half), slice(half, 2 * half))
    xs = [x[r] + 0.5 * _dot(act[r], w2_ref[...]) for r in halves]
    for r, xh in zip(halves, xs):
        gate = _sigmoid(_dot(_rms(xh, gn_ref[...]).astype(BF16), wg_ref[...]))
        o_ref[r, :] = _rms(xh + gate * e[r], fn_ref[...])


def _ffn_ple(x, p, g, w1, w3, w2, gn, wg, wp, pn, fn, tm=512):
    t = x.shape[0]
    row = pl.BlockSpec((tm, D_MODEL), lambda i: (i, 0))
    vec = _const_spec((1, D_MODEL))
    return pl.pallas_call(
        _ffn_ple_kernel,
        grid=(t // tm,),
        in_specs=[row, pl.BlockSpec((tm, PLE_DIM), lambda i: (i, 0)), vec, _const_spec((D_MODEL, D_FF)),
                  _const_spec((D_MODEL, D_FF)), _const_spec((D_FF, D_MODEL)), vec,
                  _const_spec((D_MODEL, D_MODEL)), _const_spec((PLE_DIM, D_MODEL)), vec, vec],
        out_specs=row,
        out_shape=jax.ShapeDtypeStruct((t, D_MODEL), F32),
        compiler_params=_params(1),
        name="ffn_ple",
    )(x, p, g, w1, w3, w2, gn, wg, wp, pn, fn)


def kernel(x, p, ffn1_norm, ffn1_w1, ffn1_w3, ffn1_w2, mix_norm, w_in, hgrn_lb, hgrn_onorm, w_branch_a, pool_w, pool_scale, w_branch_b, w_out, ffn2_norm, ffn2_w1, ffn2_w3, ffn2_w2, ple_norm, ple_w_gate, ple_w_proj, ple_post_norm, final_norm):
    batch, seq, d = x.shape
    assert d == D_MODEL and p.shape[0] == 1 and hgrn_lb.shape == (2, HG_WIDTH)
    assert seq % PAIR == 0
    t = batch * seq
    bf = lambda w: w.astype(BF16)
    vec = lambda g: g.reshape(1, -1)

    xt = x.reshape(t, d)
    xt, w_in_b, wa, wo, wb, pw = _ffn(
        xt, vec(ffn1_norm[0]), bf(ffn1_w1[0]), bf(ffn1_w3[0]), bf(ffn1_w2[0]),
        cast=(w_in[0], w_branch_a[0], w_out[0], w_branch_b[0], pool_w[0].reshape(POOL_WIDTH, POOL_CH)))
    xt, w1, w3, w2, wg, wp = _mixer(
        xt, vec(mix_norm[0]), w_in_b, hgrn_lb, vec(hgrn_onorm[0]), wa, pw.reshape(pool_w.shape[1:]),
        vec(pool_scale[0]), wb, wo, seq, cast=(ffn2_w1[0], ffn2_w3[0], ffn2_w2[0], ple_w_gate[0], ple_w_proj[0]))
    out = _ffn_ple(xt, p[0].reshape(t, PLE_DIM), vec(ffn2_norm[0]), w1, w3, w2,
                   vec(ple_norm[0]), wg, wp, vec(ple_post_norm[0]), vec(final_norm))
    return out.reshape(batch, seq, d)
```
